```python
import math
import jax
import jax.numpy as jnp
from jax import lax
import numpy as np

D_MODEL = 1024
BATCH = 8
SEQ = 4096
DEPTH = 2
DEC_BATCH = 32
DEC_SEQ = 4
PAST_LEN = 16384
PAGE_SIZE = 128

FOX_HEADS = 8
FOX_HD = 64
FOX_W = FOX_HEADS * FOX_HD
SB_HEADS = 8
SB_HD = 64
SB_W = SB_HEADS * SB_HD
SSM_W = D_MODEL
SSM_HD = 64
SSM_HEADS = SSM_W // SSM_HD
SSM_GROUPS = 2
SSM_STATE = 128
SSM_CONV = 4
SSM_CONV_DIM = SSM_W + 2 * SSM_GROUPS * SSM_STATE
SSM_CHUNK = 128
N_BRANCH = 3
Q_BLOCK = 128
NORM_EPS = 1e-6
IN_SPLIT_SIZES = (FOX_W, FOX_W, FOX_W, FOX_HEADS, FOX_W,
                  SB_W, SB_W, SB_W, SB_W,
                  SSM_W, SSM_CONV_DIM, SSM_HEADS,
                  N_BRANCH * D_MODEL)
IN_DIM = 4 * FOX_W + FOX_HEADS + 4 * SB_W + SSM_W + SSM_CONV_DIM + SSM_HEADS + N_BRANCH * D_MODEL

kernel_name = 'fox_stickbreak_ssd_gated_hybrid_step'


def _rms(x, g):
    xf = x.astype(jnp.float32)
    y = xf * lax.rsqrt(jnp.mean(xf * xf, axis=-1, keepdims=True) + NORM_EPS)
    return (y * g.astype(jnp.float32)).astype(x.dtype)


def _in_proj(x, norm_g, w_in):
    xn = _rms(x, norm_g)
    u = jnp.einsum('bld,de->ble', xn, w_in)
    idx, acc = [], 0
    for s in IN_SPLIT_SIZES[:-1]:
        acc += s
        idx.append(acc)
    return jnp.split(u, idx, axis=-1)


def _heads(t, n_heads, hd):
    return t.reshape(t.shape[0], t.shape[1], n_heads, hd)


def _fox_attend(q, k, v, fq, fk, qpos, kpos):
    s = jnp.einsum('bthd,bshd->bhts', q, k, preferred_element_type=jnp.float32) * (FOX_HD ** -0.5)
    s = s + fq[..., :, None] - fk[..., None, :]
    mask = kpos[None, :] <= qpos[:, None]
    p = jax.nn.softmax(jnp.where(mask, s, -jnp.inf), axis=-1)
    return jnp.einsum('bhts,bshd->bthd', p.astype(v.dtype), v)


def _sb_attend(q, k, v, qpos, kpos):
    z = jnp.einsum('bthd,bshd->bhts', q, k, preferred_element_type=jnp.float32) * (SB_HD ** -0.5)
    valid = kpos[None, :] < qpos[:, None]
    l_skip = jnp.where(valid, jax.nn.log_sigmoid(-z), 0.0)
    after = lax.cumsum(l_skip, axis=3, reverse=True) - l_skip
    a = jnp.where(valid, jnp.exp(jax.nn.log_sigmoid(z) + after), 0.0)
    return jnp.einsum('bhts,bshd->bthd', a.astype(v.dtype), v)


def _fox_prompt(q, k, v, logf):
    seq = q.shape[1]
    ft = jnp.cumsum(logf, axis=1).transpose(0, 2, 1)
    outs = []
    for i in range(seq // Q_BLOCK):
        lo, hi = i * Q_BLOCK, (i + 1) * Q_BLOCK
        outs.append(_fox_attend(q[:, lo:hi], k[:, :hi], v[:, :hi], ft[:, :, lo:hi], ft[:, :, :hi],
                                jnp.arange(lo, hi), jnp.arange(hi)))
    return jnp.concatenate(outs, axis=1)


def _sb_prompt(q, k, v):
    seq = q.shape[1]
    outs = []
    for i in range(seq // Q_BLOCK):
        lo, hi = i * Q_BLOCK, (i + 1) * Q_BLOCK
        outs.append(_sb_attend(q[:, lo:hi], k[:, :hi], v[:, :hi], jnp.arange(lo, hi), jnp.arange(hi)))
    return jnp.concatenate(outs, axis=1)


def _causal_conv(xbc, prev, w, b):
    l = xbc.shape[1]
    xp = jnp.concatenate([prev, xbc], axis=1)
    y = b
    for j in range(SSM_CONV):
        y = y + xp[:, j:j + l] * w[j]
    return jax.nn.silu(y), xp[:, -(SSM_CONV - 1):]


def _ssd(x, dt, a, bm, cm, h0):
    f32 = jnp.float32
    bsz, L = x.shape[:2]
    q = SSM_CHUNK if L % SSM_CHUNK == 0 else L
    nc = L // q
    r = SSM_HEADS // SSM_GROUPS
    xc = x.astype(f32).reshape(bsz, nc, q, SSM_GROUPS, r, SSM_HD)
    dtc = dt.reshape(bsz, nc, q, SSM_GROUPS, r)
    bc = bm.astype(f32).reshape(bsz, nc, q, SSM_GROUPS, SSM_STATE)
    cc = cm.astype(f32).reshape(bsz, nc, q, SSM_GROUPS, SSM_STATE)
    acum = jnp.cumsum(dtc * a.reshape(SSM_GROUPS, r), axis=2)
    tri = jnp.tril(jnp.ones((q, q), bool))[None, None, :, :, None, None]
    seg = acum[:, :, :, None] - acum[:, :, None, :]
    decay = jnp.exp(jnp.where(tri, seg, -jnp.inf))
    cb = jnp.einsum('bctgn,bcsgn->bctsg', cc, bc)
    w = cb[..., None] * decay * dtc[:, :, None]
    y = jnp.einsum('bctsgr,bcsgrp->bctgrp', w, xc)
    wend = jnp.exp(acum[:, :, -1:] - acum) * dtc
    st = jnp.einsum('bcsgn,bcsgr,bcsgrp->bcgrpn', bc, wend, xc)
    cdecay = jnp.exp(acum[:, :, -1])

    def step(h, inp):
        s_c, d_c = inp
        return d_c[..., None, None] * h + s_c, h

    h_init = h0.astype(f32).reshape(bsz, SSM_GROUPS, r, SSM_HD, SSM_STATE)
    h_last, h_prev = lax.scan(step, h_init, (jnp.moveaxis(st, 1, 0), jnp.moveaxis(cdecay, 1, 0)))
    h_prev = jnp.moveaxis(h_prev, 0, 1)
    y = y + jnp.einsum('bctgn,bcgrpn->bctgrp', cc, h_prev) * jnp.exp(acum)[..., None]
    y = y.reshape(bsz, L, SSM_HEADS, SSM_HD).astype(x.dtype)
    return y, h_last.reshape(bsz, SSM_HEADS, SSM_HD, SSM_STATE).astype(h0.dtype)


def _ssm_branch(z, xbc, dt_raw, conv_prev, h0, conv_w, conv_b, dt_bias, a_log, d_skip, norm_w):
    b, l = z.shape[:2]
    xbc_act, conv_new = _causal_conv(xbc, conv_prev, conv_w, conv_b)
    xs, bm, cm = jnp.split(xbc_act, [SSM_W, SSM_W + SSM_GROUPS * SSM_STATE], axis=-1)
    xs = _heads(xs, SSM_HEADS, SSM_HD)
    bm = bm.reshape(b, l, SSM_GROUPS, SSM_STATE)
    cm = cm.reshape(b, l, SSM_GROUPS, SSM_STATE)
    dt = jax.nn.softplus((dt_raw + dt_bias).astype(jnp.float32))
    a = -jnp.exp(a_log.astype(jnp.float32))
    y, h_last = _ssd(xs, dt, a, bm, cm, h0)
    y = y + xs * d_skip[:, None]
    y = y.reshape(b, l, SSM_W) * jax.nn.silu(z)
    y = _rms(y.reshape(b, l, SSM_GROUPS, SSM_W // SSM_GROUPS),
             norm_w.reshape(SSM_GROUPS, SSM_W // SSM_GROUPS)).reshape(b, l, SSM_W)
    return y, conv_new, h_last


def _merge(x, ya, yb, yc, gate_logit, b_gate, w_pa, w_pb, w_pc, w_out):
    g = jax.nn.sigmoid((gate_logit + b_gate).astype(jnp.float32)).astype(x.dtype)
    ga, gb, gc = jnp.split(g, N_BRANCH, axis=-1)
    h = ga * (ya @ w_pa) + gb * (yb @ w_pb) + gc * (yc @ w_pc)
    return x + h @ w_out


def _project_layer(x, l, p):
    (fq, fk, fv, ff, fg, sq, sk, sv, sg, cz, cxbc, cdt, gl) = _in_proj(x, p['norm_g'][l], p['w_in'][l])
    q = _rms(_heads(fq, FOX_HEADS, FOX_HD), p['fox_q_norm'][l])
    k = _rms(_heads(fk, FOX_HEADS, FOX_HD), p['fox_k_norm'][l])
    v = _heads(fv, FOX_HEADS, FOX_HD)
    logf = jax.nn.log_sigmoid((ff + p['b_forget'][l]).astype(jnp.float32))
    sq = _heads(sq, SB_HEADS, SB_HD)
    sk = _heads(sk, SB_HEADS, SB_HD)
    sv = _heads(sv, SB_HEADS, SB_HD)
    return q, k, v, logf, fg, sq, sk, sv, sg, cz, cxbc, cdt, gl


def _finish_layer(x, l, p, oa, fg, ob, sg, cz, cxbc, cdt, gl, conv_prev, h0):
    b, n = x.shape[:2]
    ya = oa.reshape(b, n, FOX_W) * jax.nn.silu(fg)
    yb = ob.reshape(b, n, SB_W) * jax.nn.silu(sg)
    yc, conv_new, h_last = _ssm_branch(cz, cxbc, cdt, conv_prev, h0, p['conv_w'][l], p['conv_b'][l],
                                       p['dt_bias'][l], p['a_log'][l], p['d_skip'][l], p['ssm_norm'][l])
    x = _merge(x, ya, yb, yc, gl, p['b_gate'][l], p['w_pa'][l], p['w_pb'][l], p['w_pc'][l], p['w_out'][l])
    return x, conv_new, h_last


def _layer_prompt(x, l, p):
    b = x.shape[0]
    q, k, v, logf, fg, sq, sk, sv, sg, cz, cxbc, cdt, gl = _project_layer(x, l, p)
    oa = _fox_prompt(q, k, v, logf)
    ob = _sb_prompt(sq, sk, sv)
    conv0 = jnp.zeros((b, SSM_CONV - 1, SSM_CONV_DIM), x.dtype)
    h0 = jnp.zeros((b, SSM_HEADS, SSM_HD, SSM_STATE), x.dtype)
    x, conv_new, h_last = _finish_layer(x, l, p, oa, fg, ob, sg, cz, cxbc, cdt, gl, conv0, h0)
    return x, (k, v, logf, sk, sv, h_last, conv_new)


def _layer_sample(x, l, p, cache_fox_k, cache_fox_v, cache_fox_logf, cache_sb_k, cache_sb_v,
                  state_ssm, state_conv, page_table):
    n_new = x.shape[1]
    past = page_table.shape[1] * PAGE_SIZE

    def gather(pool):
        g = pool[page_table]
        return g.reshape((g.shape[0], past) + g.shape[3:])

    q, k, v, logf, fg, sq, sk, sv, sg, cz, cxbc, cdt, gl = _project_layer(x, l, p)
    qpos = past + jnp.arange(n_new)
    kpos = jnp.arange(past + n_new)
    k_all = jnp.concatenate([gather(cache_fox_k[l]), k], axis=1)
    v_all = jnp.concatenate([gather(cache_fox_v[l]), v], axis=1)
    lf_all = jnp.concatenate([gather(cache_fox_logf[l]).astype(jnp.float32), logf], axis=1)
    ft = jnp.cumsum(lf_all, axis=1).transpose(0, 2, 1)
    oa = _fox_attend(q, k_all, v_all, ft[:, :, past:], ft, qpos, kpos)
    sk_all = jnp.concatenate([gather(cache_sb_k[l]), sk], axis=1)
    sv_all = jnp.concatenate([gather(cache_sb_v[l]), sv], axis=1)
    ob = _sb_attend(sq, sk_all, sv_all, qpos, kpos)
    x, conv_new, h_last = _finish_layer(x, l, p, oa, fg, ob, sg, cz, cxbc, cdt, gl,
                                        state_conv[l], state_ssm[l])
    return x, (k, v, logf, sk, sv, h_last, conv_new)


def setup_inputs(seed: int = 0) -> dict:
    key = jax.random.key(seed)
    ks = jax.random.split(key, 32)
    f32 = jnp.float32
    nrm = lambda i, shape: jax.random.normal(ks[i], shape, f32)
    n_pages = PAST_LEN // PAGE_SIZE
    n_used = DEC_BATCH * n_pages
    n_pool = n_used + max(1, n_used // 4)
    page_table = jax.random.permutation(ks[0], n_pool)[:n_used].reshape(DEC_BATCH, n_pages).astype(jnp.int32)
    dt0 = jnp.exp(jax.random.uniform(ks[1], (DEPTH, SSM_HEADS), f32) * (math.log(0.1) - math.log(1e-3)) + math.log(1e-3))
    return {
        'x_prompt': nrm(2, (BATCH, SEQ, D_MODEL)),
        'x_sample': nrm(3, (DEC_BATCH, DEC_SEQ, D_MODEL)),
        'cache_fox_k': nrm(4, (DEPTH, n_pool, PAGE_SIZE, FOX_HEADS, FOX_HD)),
        'cache_fox_v': nrm(5, (DEPTH, n_pool, PAGE_SIZE, FOX_HEADS, FOX_HD)),
        'cache_fox_logf': jax.nn.log_sigmoid(3.0 + nrm(6, (DEPTH, n_pool, PAGE_SIZE, FOX_HEADS))),
        'cache_sb_k': nrm(7, (DEPTH, n_pool, PAGE_SIZE, SB_HEADS, SB_HD)),
        'cache_sb_v': nrm(8, (DEPTH, n_pool, PAGE_SIZE, SB_HEADS, SB_HD)),
        'state_ssm': 0.1 * nrm(9, (DEPTH, DEC_BATCH, SSM_HEADS, SSM_HD, SSM_STATE)),
        'state_conv': nrm(10, (DEPTH, DEC_BATCH, SSM_CONV - 1, SSM_CONV_DIM)),
        'page_table': page_table,
        'norm_g': 1.0 + 0.02 * nrm(11, (DEPTH, D_MODEL)),
        'w_in': nrm(12, (DEPTH, D_MODEL, IN_DIM)) * D_MODEL ** -0.5,
        'b_forget': 3.0 + 0.1 * nrm(13, (DEPTH, FOX_HEADS)),
        'fox_q_norm': 1.0 + 0.02 * nrm(14, (DEPTH, FOX_HD)),
        'fox_k_norm': 1.0 + 0.02 * nrm(15, (DEPTH, FOX_HD)),
        'conv_w': 0.5 * nrm(16, (DEPTH, SSM_CONV, SSM_CONV_DIM)),
        'conv_b': 0.01 * nrm(17, (DEPTH, SSM_CONV_DIM)),
        'dt_bias': dt0 + jnp.log(-jnp.expm1(-dt0)),
        'a_log': jnp.log(jax.random.uniform(ks[18], (DEPTH, SSM_HEADS), f32, 1.0, 16.0)),
        'd_skip': 1.0 + 0.1 * nrm(19, (DEPTH, SSM_HEADS)),
        'ssm_norm': 1.0 + 0.02 * nrm(20, (DEPTH, SSM_W)),
        'b_gate': 0.01 * nrm(21, (DEPTH, N_BRANCH * D_MODEL)),
        'w_pa': nrm(22, (DEPTH, FOX_W, D_MODEL)) * FOX_W ** -0.5,
        'w_pb': nrm(23, (DEPTH, SB_W, D_MODEL)) * SB_W ** -0.5,
        'w_pc': nrm(24, (DEPTH, SSM_W, D_MODEL)) * SSM_W ** -0.5,
        'w_out': nrm(25, (DEPTH, D_MODEL, D_MODEL)) * D_MODEL ** -0.5,
    }


def reference(x_prompt, x_sample, cache_fox_k, cache_fox_v, cache_fox_logf, cache_sb_k, cache_sb_v,
              state_ssm, state_conv, page_table, norm_g, w_in, b_forget, fox_q_norm, fox_k_norm,
              conv_w, conv_b, dt_bias, a_log, d_skip, ssm_norm, b_gate, w_pa, w_pb, w_pc, w_out):
    p = dict(norm_g=norm_g, w_in=w_in, b_forget=b_forget, fox_q_norm=fox_q_norm, fox_k_norm=fox_k_norm,
             conv_w=conv_w, conv_b=conv_b, dt_bias=dt_bias, a_log=a_log, d_skip=d_skip,
             ssm_norm=ssm_norm, b_gate=b_gate, w_pa=w_pa, w_pb=w_pb, w_pc=w_pc, w_out=w_out)
    xp, xs = x_prompt, x_sample
    new_p, new_s = [], []
    for l in range(DEPTH):
        xp, st_p = _layer_prompt(xp, l, p)
        xs, st_s = _layer_sample(xs, l, p, cache_fox_k, cache_fox_v, cache_fox_logf, cache_sb_k,
                                 cache_sb_v, state_ssm, state_conv, page_table)
        new_p.append(st_p)
        new_s.append(st_s)
    fk_p, fv_p, fl_p, sk_p, sv_p, ssm_p, conv_p = [jnp.stack([s[i] for s in new_p]) for i in range(7)]
    fk_s, fv_s, fl_s, sk_s, sv_s, ssm_s, conv_s = [jnp.stack([s[i] for s in new_s]) for i in range(7)]
    return (xp, xs, fk_p, fv_p, fl_p, sk_p, sv_p, ssm_p, conv_p,
            fk_s, fv_s, fl_s, sk_s, sv_s, ssm_s, conv_s)
```

```python
import functools

import numpy as np
import jax
import jax.numpy as jnp
from jax import lax
from jax.experimental import pallas as pl
from jax.experimental.pallas import tpu as pltpu

F32 = jnp.float32
BF16 = jnp.bfloat16

D_MODEL = 1024
HEADS = 8
HD = 64
AW = HEADS * HD
SSM_W = 1024
SSM_HEADS = 16
SSM_HD = 64
SSM_GROUPS = 2
SSM_STATE = 128
SSM_CONV = 4
SSM_CONV_DIM = SSM_W + 2 * SSM_GROUPS * SSM_STATE
SSM_CHUNK = 128
PAGE = 128
NORM_EPS = 1e-6
NEG_INF = float("-inf")

LANES = 128
SUBLANES = 8
VMEM_LIMIT = 56 * 1024 * 1024

C_FQ, C_FK, C_FV, C_FG = 0, 512, 1024, 1536
C_SQ, C_SK, C_SV, C_SG = 2048, 2560, 3072, 3584
C_SMALL = 4096
C_XBC = 4608
C_Z = 6144
C_GL = 7168
U_W = 10240
DT_LANE0 = 8
PAGES_PER_STEP = 8

ATT_TQ = 512
ATT_TK = 512


def _cparams(sem):
    return pltpu.CompilerParams(dimension_semantics=sem, vmem_limit_bytes=VMEM_LIMIT)


def _logsig(x):
    return jnp.minimum(x, 0.0) - jnp.log1p(jnp.exp(-jnp.abs(x)))


def _softplus(x):
    return jnp.maximum(x, 0.0) + jnp.log1p(jnp.exp(-jnp.abs(x)))


def _sigmoid(x):
    return 1.0 / (1.0 + jnp.exp(-x))


def _silu(x):
    return x * _sigmoid(x)


def _split3(x):
    hi = x.astype(BF16)
    r1 = x - hi.astype(F32)
    mid = r1.astype(BF16)
    lo = (r1 - mid.astype(F32)).astype(BF16)
    return hi, mid, lo


def _split2(x):
    hi = x.astype(BF16)
    lo = (x - hi.astype(F32)).astype(BF16)
    return hi, lo


def _dot(a, b):
    return jnp.dot(a, b, preferred_element_type=F32)


def _dot_nt(a, b):
    return lax.dot_general(a, b, (((1,), (1,)), ((), ())), preferred_element_type=F32)


def _dot3(pieces, m):
    out = _dot(pieces[0], m)
    for p in pieces[1:]:
        out = out + _dot(p, m)
    return out


def _in_proj_kernel(x_ref, g_ref, w_ref, u_ref, xn_ref):
    @pl.when(pl.program_id(1) == 0)
    def _():
        x = x_ref[...]
        ms = jnp.mean(x * x, axis=-1, keepdims=True)
        xn_ref[...] = (x * lax.rsqrt(ms + NORM_EPS) * g_ref[...]).astype(BF16)

    u_ref[...] = _dot(xn_ref[...], w_ref[...])


def _in_proj(x2d, g, w_pad, tm, tn=1024):
    n = x2d.shape[0]
    return pl.pallas_call(
        _in_proj_kernel,
        grid=(n // tm, U_W // tn),
        in_specs=[
            pl.BlockSpec((tm, D_MODEL), lambda i, j: (i, 0)),
            pl.BlockSpec((1, D_MODEL), lambda i, j: (0, 0)),
            pl.BlockSpec((D_MODEL, tn), lambda i, j: (0, j)),
        ],
        out_specs=pl.BlockSpec((tm, tn), lambda i, j: (i, j)),
        out_shape=jax.ShapeDtypeStruct((n, U_W), F32),
        scratch_shapes=[pltpu.VMEM((tm, D_MODEL), BF16)],
        compiler_params=_cparams(("arbitrary", "arbitrary")),
        name="in_proj",
    )(x2d, g, w_pad)


def _head_norm(x, gain, bd):
    xx = x * x
    hi, lo = _split2(xx)
    ss = _dot(hi, bd) + _dot(lo, bd)
    return x * lax.rsqrt(ss * (1.0 / HD) + NORM_EPS) * gain


def _half_mask(shape, head):
    lane = lax.broadcasted_iota(jnp.int32, shape, len(shape) - 1)
    return (lane >= HD) if (head % 2) else (lane < HD)


def _prep_prompt_kernel(fq_ref, fk_ref, fv_ref, sq_ref, sk_ref, sv_ref, sm_ref,
                        gq_ref, gk_ref, bf_ref, bd_ref, tri_ref, sqm_ref, skm_ref, oq_ref, ok_ref,
                        kt_ref, vt_ref, skt_ref, svt_ref, lft_ref, qa_ref, ka_ref, vtf_ref, qs_ref, ks_ref,
                        vts_ref, carry_ref):
    @pl.when(pl.program_id(1) == 0)
    def _():
        carry_ref[...] = jnp.zeros_like(carry_ref)

    bd = bd_ref[...]
    q = _head_norm(fq_ref[...], gq_ref[...], bd)
    k = _head_norm(fk_ref[...], gk_ref[...], bd)

    sm = sm_ref[...]
    lane = lax.broadcasted_iota(jnp.int32, sm.shape, 1)
    lf = jnp.where(lane < HEADS, _logsig(sm + bf_ref[...]), 0.0)
    lft_ref[0] = lf.T[0:HEADS, :]

    tri = tri_ref[...]
    lf3 = _split3(lf)
    cum = _dot(tri, lf3[0]) + _dot(tri, lf3[1]) + _dot(tri, lf3[2]) + carry_ref[...]
    tm = cum.shape[0]
    carry_ref[...] = cum[tm - 1:tm, :]

    cat = jnp.concatenate(_split3(cum), axis=1)
    aug_q = (_dot(cat, sqm_ref[...]) + oq_ref[...]).astype(BF16)
    aug_k = (_dot(cat, skm_ref[...]) + ok_ref[...]).astype(BF16)

    qb = q.astype(BF16)
    kb = k.astype(BF16)
    fv = fv_ref[...]
    sq = (sq_ref[...] * (HD ** -0.5)).astype(BF16)
    sk32 = sk_ref[...]
    sk = sk32.astype(BF16)
    sv = sv_ref[...]
    qs_ref[0] = sq
    zero = jnp.zeros((tm, LANES), BF16)
    for j in range(HEADS // 2):
        ls = slice(LANES * j, LANES * (j + 1))
        qa_ref[0, :, 2 * LANES * j: 2 * LANES * j + LANES] = qb[:, ls]
        qa_ref[0, :, 2 * LANES * j + LANES: 2 * LANES * (j + 1)] = aug_q[:, ls]
        fvt = fv[:, ls].T
        svt = sv[:, ls].T
        vt_ref[0, ls, :] = fvt
        svt_ref[0, ls, :] = svt
        vtf_ref[0, ls, :] = fvt.astype(BF16)
        vts_ref[0, ls, :] = svt.astype(BF16)
        kt_ref[0, ls, :] = k[:, ls].T
        skt_ref[0, ls, :] = sk32[:, ls].T
        for e in range(2):
            h = 2 * j + e
            hm = _half_mask((tm, LANES), h)
            ka_ref[0, :, 2 * LANES * h: 2 * LANES * h + LANES] = jnp.where(hm, kb[:, ls], zero)
            ka_ref[0, :, 2 * LANES * h + LANES: 2 * LANES * (h + 1)] = aug_k[:, LANES * h: LANES * (h + 1)]
            ks_ref[0, :, LANES * h: LANES * (h + 1)] = jnp.where(hm, sk[:, ls], zero)


def _prep_consts(tm):
    bd = np.kron(np.eye(HEADS, dtype=np.float32), np.ones((HD, HD), np.float32))
    tri = np.tril(np.ones((tm, tm), np.float32))
    sqm = np.zeros((3 * LANES, AW), np.float32)
    skm = np.zeros((3 * LANES, 2 * AW), np.float32)
    oq = np.zeros((1, AW), np.float32)
    ok = np.zeros((1, 2 * AW), np.float32)
    for h in range(HEADS):
        base = (h % 2) * 6
        for i in range(3):
            sqm[i * LANES + h, (h // 2) * LANES + base + i] = 1.0
            skm[i * LANES + h, h * LANES + base + 3 + i] = -1.0
            oq[0, (h // 2) * LANES + base + 3 + i] = 1.0
            ok[0, h * LANES + base + i] = 1.0
    return (jnp.asarray(bd, BF16), jnp.asarray(tri, BF16), jnp.asarray(sqm, BF16),
            jnp.asarray(skm, BF16), jnp.asarray(oq), jnp.asarray(ok))


def _prep_prompt(u, b, s, gq, gk, bf, tm=512):
    nt = s // tm
    bd, tri, sqm, skm, oq, ok = _prep_consts(tm)

    def ucol(c, w):
        return pl.BlockSpec((tm, w), lambda bi, i, c=c, w=w: (bi * nt + i, c // w))

    def const(shape):
        return pl.BlockSpec(shape, lambda bi, i: (0,) * len(shape))

    outs = pl.pallas_call(
        _prep_prompt_kernel,
        grid=(b, nt),
        in_specs=[ucol(C_FQ, AW), ucol(C_FK, AW), ucol(C_FV, AW), ucol(C_SQ, AW), ucol(C_SK, AW),
                  ucol(C_SV, AW), ucol(C_SMALL, LANES),
                  const((1, AW)), const((1, AW)), const((1, LANES)), const((AW, AW)), const((tm, tm)),
                  const((3 * LANES, AW)), const((3 * LANES, 2 * AW)), const((1, AW)), const((1, 2 * AW))],
        out_specs=[
            pl.BlockSpec((1, AW, tm), lambda bi, i: (bi, 0, i)),
            pl.BlockSpec((1, AW, tm), lambda bi, i: (bi, 0, i)),
            pl.BlockSpec((1, AW, tm), lambda bi, i: (bi, 0, i)),
            pl.BlockSpec((1, AW, tm), lambda bi, i: (bi, 0, i)),
            pl.BlockSpec((1, HEADS, tm), lambda bi, i: (bi, 0, i)),
            pl.BlockSpec((1, tm, 2 * AW), lambda bi, i: (bi, i, 0)),
            pl.BlockSpec((1, tm, 4 * AW), lambda bi, i: (bi, i, 0)),
            pl.BlockSpec((1, AW, tm), lambda bi, i: (bi, 0, i)),
            pl.BlockSpec((1, tm, AW), lambda bi, i: (bi, i, 0)),
            pl.BlockSpec((1, tm, 2 * AW), lambda bi, i: (bi, i, 0)),
            pl.BlockSpec((1, AW, tm), lambda bi, i: (bi, 0, i)),
        ],
        out_shape=[
            jax.ShapeDtypeStruct((b, AW, s), F32),
            jax.ShapeDtypeStruct((b, AW, s), F32),
            jax.ShapeDtypeStruct((b, AW, s), F32),
            jax.ShapeDtypeStruct((b, AW, s), F32),
            jax.ShapeDtypeStruct((b, HEADS, s), F32),
            jax.ShapeDtypeStruct((b, s, 2 * AW), BF16),
            jax.ShapeDtypeStruct((b, s, 4 * AW), BF16),
            jax.ShapeDtypeStruct((b, AW, s), BF16),
            jax.ShapeDtypeStruct((b, s, AW), BF16),
            jax.ShapeDtypeStruct((b, s, 2 * AW), BF16),
            jax.ShapeDtypeStruct((b, AW, s), BF16),
        ],
        scratch_shapes=[pltpu.VMEM((1, LANES), F32)],
        compiler_params=_cparams(("arbitrary", "arbitrary")),
        name="prep_prompt",
    )(u, u, u, u, u, u, u, gq, gk, bf, bd, tri, sqm, skm, oq, ok)
    return outs


def _prep_sample_kernel(fq_ref, fk_ref, sm_ref, gq_ref, gk_ref, bf_ref, bd_ref,
                        qhat_ref, khat_ref, logf_ref):
    bd = bd_ref[...]
    qhat_ref[...] = _head_norm(fq_ref[...], gq_ref[...], bd)
    khat_ref[...] = _head_norm(fk_ref[...], gk_ref[...], bd)
    sm = sm_ref[...]
    lane = lax.broadcasted_iota(jnp.int32, sm.shape, 1)
    logf_ref[...] = jnp.where(lane < HEADS, _logsig(sm + bf_ref[...]), 0.0)


def _prep_sample(u, gq, gk, bf):
    n = u.shape[0]
    bd = _prep_consts(8)[0]

    def ucol(c, w):
        return pl.BlockSpec((n, w), lambda i, c=c, w=w: (0, c // w))

    def const(shape):
        return pl.BlockSpec(shape, lambda i: (0,) * len(shape))

    return pl.pallas_call(
        _prep_sample_kernel,
        grid=(1,),
        in_specs=[ucol(C_FQ, AW), ucol(C_FK, AW), ucol(C_SMALL, LANES),
                  const((1, AW)), const((1, AW)), const((1, LANES)), const((AW, AW))],
        out_specs=[pl.BlockSpec((n, AW), lambda i: (0, 0)), pl.BlockSpec((n, AW), lambda i: (0, 0)),
                   pl.BlockSpec((n, LANES), lambda i: (0, 0))],
        out_shape=[jax.ShapeDtypeStruct((n, AW), F32), jax.ShapeDtypeStruct((n, AW), F32),
                   jax.ShapeDtypeStruct((n, LANES), F32)],
        compiler_params=_cparams(("arbitrary",)),
        name="prep_sample",
    )(u, u, u, gq, gk, bf, bd)


def _fox_kernel(qi_ref, ki_ref, qa_ref, ka_ref, vt_ref, o_ref, m_ref, l_ref, acc_ref):
    s = pl.program_id(2)
    qi = qi_ref[s]
    ki = ki_ref[s]

    @pl.when(ki == 0)
    def _():
        m_ref[...] = jnp.full_like(m_ref, NEG_INF)
        l_ref[...] = jnp.zeros_like(l_ref)
        acc_ref[...] = jnp.zeros_like(acc_ref)

    def body(masked):
        q = qa_ref[0]
        for e in range(2):
            k = ka_ref[0, :, 2 * LANES * e: 2 * LANES * (e + 1)]
            st = _dot_nt(k, q)
            if masked:
                r = lax.broadcasted_iota(jnp.int32, st.shape, 0)
                c = lax.broadcasted_iota(jnp.int32, st.shape, 1)
                st = jnp.where(r <= c, st, NEG_INF)
            m_prev = m_ref[e:e + 1, :]
            m_new = jnp.maximum(m_prev, jnp.max(st, axis=0, keepdims=True))
            alpha = jnp.exp(m_prev - m_new)
            p = jnp.exp(st - m_new)
            l_ref[e:e + 1, :] = alpha * l_ref[e:e + 1, :] + jnp.sum(p, axis=0, keepdims=True)
            m_ref[e:e + 1, :] = m_new
            pv = _dot(vt_ref[0, HD * e: HD * (e + 1), :], p.astype(BF16))
            acc_ref[HD * e: HD * (e + 1), :] = alpha * acc_ref[HD * e: HD * (e + 1), :] + pv

    @pl.when(ki < qi)
    def _():
        body(False)

    @pl.when(ki == qi)
    def _():
        body(True)
        o0 = acc_ref[0:HD, :] / l_ref[0:1, :]
        o1 = acc_ref[HD:2 * HD, :] / l_ref[1:2, :]
        o_ref[0] = jnp.concatenate([o0, o1], axis=0).T


def _sb_kernel(qi_ref, ki_ref, mt2_ref, q_ref, k_ref, vt_ref, o_ref, carry_ref, acc_ref):
    s = pl.program_id(2)
    qi = qi_ref[s]
    ki = ki_ref[s]

    @pl.when(ki == qi)
    def _():
        carry_ref[...] = jnp.zeros_like(carry_ref)
        acc_ref[...] = jnp.zeros_like(acc_ref)

    mt2 = mt2_ref[...]
    tk = k_ref.shape[1]
    nsub = tk // LANES

    def body(masked):
        q = q_ref[0]
        for e in range(2):
            k = k_ref[0, :, LANES * e: LANES * (e + 1)]
            zt = _dot_nt(k, q)
            carry = carry_ref[e:e + 1, :]
            acc = acc_ref[HD * e: HD * (e + 1), :]
            for sub in range(nsub - 1, -1, -1):
                z = zt[LANES * sub: LANES * (sub + 1), :]
                lp = jnp.log1p(jnp.exp(-jnp.abs(z)))
                lsig = jnp.minimum(z, 0.0) - lp
                lskip = lsig - z
                if masked:
                    r = lax.broadcasted_iota(jnp.int32, z.shape, 0) + LANES * sub
                    c = lax.broadcasted_iota(jnp.int32, z.shape, 1)
                    valid = r < c
                    lskip = jnp.where(valid, lskip, 0.0)
                hi, lo = _split2(lskip)
                after = _dot(mt2, jnp.concatenate([hi, lo], axis=0)) + carry
                a = jnp.exp(lsig + after)
                if masked:
                    a = jnp.where(valid, a, 0.0)
                acc = acc + _dot(vt_ref[0, HD * e: HD * (e + 1), LANES * sub: LANES * (sub + 1)],
                                 a.astype(BF16))
                carry = carry + jnp.sum(lskip, axis=0, keepdims=True)
            carry_ref[e:e + 1, :] = carry
            acc_ref[HD * e: HD * (e + 1), :] = acc

    @pl.when(ki < qi)
    def _():
        body(False)

    @pl.when(ki == qi)
    def _():
        body(True)

    @pl.when(ki == 0)
    def _():
        o_ref[0] = acc_ref[...].T


def _tri_steps(nq, descending):
    qs, ks = [], []
    for qi in range(nq):
        order = range(qi, -1, -1) if descending else range(qi + 1)
        for ki in order:
            qs.append(qi)
            ks.append(ki)
    return jnp.asarray(qs, jnp.int32), jnp.asarray(ks, jnp.int32)


def _fox_attention(qa, ka, vt):
    b, s, _ = qa.shape
    tq, tk = ATT_TQ, ATT_TK
    qs, ks = _tri_steps(s // tq, descending=False)
    grid_spec = pltpu.PrefetchScalarGridSpec(
        num_scalar_prefetch=2,
        grid=(b, HEADS // 2, qs.shape[0]),
        in_specs=[
            pl.BlockSpec((1, tq, 2 * LANES), lambda bi, j, st, qr, kr: (bi, qr[st], j)),
            pl.BlockSpec((1, tk, 4 * LANES), lambda bi, j, st, qr, kr: (bi, kr[st], j)),
            pl.BlockSpec((1, LANES, tk), lambda bi, j, st, qr, kr: (bi, j, kr[st])),
        ],
        out_specs=pl.BlockSpec((1, tq, LANES), lambda bi, j, st, qr, kr: (bi, qr[st], j)),
        scratch_shapes=[pltpu.VMEM((SUBLANES, tq), F32), pltpu.VMEM((SUBLANES, tq), F32),
                        pltpu.VMEM((LANES, tq), F32)],
    )
    return pl.pallas_call(
        _fox_kernel, grid_spec=grid_spec,
        out_shape=jax.ShapeDtypeStruct((b, s, AW), F32),
        compiler_params=_cparams(("arbitrary", "arbitrary", "arbitrary")),
        name="fox_attention",
    )(qs, ks, qa, ka, vt)


def _sb_consts():
    m = np.triu(np.ones((LANES, LANES), np.float32), 1)
    return jnp.asarray(np.concatenate([m, m], axis=1), BF16)


def _sb_attention(q, k, vt):
    b, s, _ = q.shape
    tq, tk = ATT_TQ, ATT_TK
    qs, ks = _tri_steps(s // tq, descending=True)
    grid_spec = pltpu.PrefetchScalarGridSpec(
        num_scalar_prefetch=2,
        grid=(b, HEADS // 2, qs.shape[0]),
        in_specs=[
            pl.BlockSpec((LANES, 2 * LANES), lambda bi, j, st, qr, kr: (0, 0)),
            pl.BlockSpec((1, tq, LANES), lambda bi, j, st, qr, kr: (bi, qr[st], j)),
            pl.BlockSpec((1, tk, 2 * LANES), lambda bi, j, st, qr, kr: (bi, kr[st], j)),
            pl.BlockSpec((1, LANES, tk), lambda bi, j, st, qr, kr: (bi, j, kr[st])),
        ],
        out_specs=pl.BlockSpec((1, tq, LANES), lambda bi, j, st, qr, kr: (bi, qr[st], j)),
        scratch_shapes=[pltpu.VMEM((SUBLANES, tq), F32), pltpu.VMEM((LANES, tq), F32)],
    )
    return pl.pallas_call(
        _sb_kernel, grid_spec=grid_spec,
        out_shape=jax.ShapeDtypeStruct((b, s, AW), F32),
        compiler_params=_cparams(("arbitrary", "arbitrary", "arbitrary")),
        name="sb_attention",
    )(qs, ks, _sb_consts(), q, k, vt)


def _decode_kernel(pt_ref, *refs, layer, n_new):
    npg = PAGES_PER_STEP
    fk_refs = refs[0:npg]
    fv_refs = refs[npg:2 * npg]
    lf_refs = refs[2 * npg:3 * npg]
    sk_refs = refs[3 * npg:4 * npg]
    sv_refs = refs[4 * npg:5 * npg]
    (qhat_ref, khat_ref, lfn_ref, vnew_ref, sq_ref, sknew_ref, svnew_ref, mst_ref,
     oa_ref, ob_ref,
     qf_ref, qsb_ref, m_ref, l_ref, cf_ref, cs_ref, accf_ref, accs_ref) = refs[5 * npg:]
    c = pl.program_id(1)
    rows = HEADS * n_new
    mst = mst_ref[...]
    mst3 = jnp.concatenate([mst, mst, mst], axis=0)
    mst2 = jnp.concatenate([mst, mst], axis=0)

    def q_rows(x):
        parts = [jnp.broadcast_to(x[t:t + 1, :], (HEADS, AW)) for t in range(n_new)]
        xr = jnp.concatenate(parts, axis=0)
        r = lax.broadcasted_iota(jnp.int32, xr.shape, 0)
        ln = lax.broadcasted_iota(jnp.int32, xr.shape, 1)
        return jnp.where((ln // HD) == (r % HEADS), xr, 0.0)

    def pad_rows(x):
        return jnp.concatenate([x, jnp.zeros((PAGE - n_new, x.shape[1]), x.dtype)], axis=0)

    def fox_update(st, bias_valid, v_bf, v_transposed):
        st = jnp.where(bias_valid, st, NEG_INF) if bias_valid is not None else st
        m_prev = m_ref[...]
        m_new = jnp.maximum(m_prev, jnp.max(st, axis=1, keepdims=True))
        alpha = jnp.exp(m_prev - m_new)
        p = jnp.exp(st - m_new)
        l_ref[...] = alpha * l_ref[...] + jnp.sum(p, axis=1, keepdims=True)
        m_ref[...] = m_new
        pv = _dot_nt(p.astype(BF16), v_bf) if v_transposed else _dot(p.astype(BF16), v_bf)
        accf_ref[...] = alpha * accf_ref[...] + pv

    def sb_update(z, valid, v_bf, v_transposed):
        lp = jnp.log1p(jnp.exp(-jnp.abs(z)))
        lsig = jnp.minimum(z, 0.0) - lp
        lskip = lsig - z
        if valid is not None:
            lskip = jnp.where(valid, lskip, 0.0)
        hi, lo = _split2(lskip)
        after = _dot(jnp.concatenate([hi, lo], axis=1), mst2) + cs_ref[...]
        a = jnp.exp(lsig + after)
        if valid is not None:
            a = jnp.where(valid, a, 0.0)
        av = _dot_nt(a.astype(BF16), v_bf) if v_transposed else _dot(a.astype(BF16), v_bf)
        accs_ref[...] = accs_ref[...] + av
        cs_ref[...] = cs_ref[...] + jnp.sum(lskip, axis=1, keepdims=True)

    @pl.when(c == 0)
    def _():
        qf = q_rows(qhat_ref[0]).astype(BF16)
        qsb = q_rows(sq_ref[0] * (HD ** -0.5)).astype(BF16)
        qf_ref[...] = qf
        qsb_ref[...] = qsb
        m_ref[...] = jnp.full_like(m_ref, NEG_INF)
        l_ref[...] = jnp.zeros_like(l_ref)
        cs_ref[...] = jnp.zeros_like(cs_ref)
        accf_ref[...] = jnp.zeros_like(accf_ref)
        accs_ref[...] = jnp.zeros_like(accs_ref)
        r = lax.broadcasted_iota(jnp.int32, (rows, PAGE), 0)
        j = lax.broadcasted_iota(jnp.int32, (rows, PAGE), 1)
        t = r // HEADS
        lfn = lfn_ref[0]
        hl = lax.broadcasted_iota(jnp.int32, (HEADS, LANES), 0)
        ll = lax.broadcasted_iota(jnp.int32, (HEADS, LANES), 1)
        cols = []
        for i in range(n_new):
            sel = jnp.sum(jnp.where(hl == ll, jnp.broadcast_to(lfn[i:i + 1, :], (HEADS, LANES)), 0.0),
                          axis=1, keepdims=True)
            cols.append(jnp.concatenate([sel] * n_new, axis=0))
        bias = jnp.zeros((rows, PAGE), F32)
        tot = jnp.zeros((rows, 1), F32)
        for i in range(n_new):
            bias = bias + jnp.where((j < i) & (i <= t), cols[i], 0.0)
            tot = tot + jnp.where(i <= t[:, 0:1], cols[i], 0.0)
        cf_ref[...] = tot
        st = _dot_nt(qf, pad_rows(khat_ref[0]).astype(BF16)) + bias
        fox_update(st, j <= t, pad_rows(vnew_ref[0]).astype(BF16), False)
        z = _dot_nt(qsb, pad_rows(sknew_ref[0]).astype(BF16))
        sb_update(z, j < t, pad_rows(svnew_ref[0]).astype(BF16), False)

    qf = qf_ref[...]
    qsb = qsb_ref[...]
    for i in range(npg):
        st = _dot(qf, fk_refs[i][0, 0].astype(BF16))
        lf8 = lf_refs[i][0, 0]
        lf = jnp.concatenate([lf8] * n_new, axis=0)
        suffix = _dot(jnp.concatenate(_split3(lf), axis=1), mst3)
        st = st + (suffix + cf_ref[...])
        cf_ref[...] = cf_ref[...] + jnp.sum(lf, axis=1, keepdims=True)
        fox_update(st, None, fv_refs[i][0, 0].astype(BF16), True)
        z = _dot(qsb, sk_refs[i][0, 0].astype(BF16))
        sb_update(z, None, sv_refs[i][0, 0].astype(BF16), True)

    @pl.when(c == pl.num_programs(1) - 1)
    def _():
        r = lax.broadcasted_iota(jnp.int32, (rows, AW), 0)
        ln = lax.broadcasted_iota(jnp.int32, (rows, AW), 1)
        sel = (ln // HD) == (r % HEADS)
        of = jnp.where(sel, accf_ref[...] / l_ref[...], 0.0)
        os_ = jnp.where(sel, accs_ref[...], 0.0)
        for t in range(n_new):
            oa_ref[0, t:t + 1, :] = jnp.sum(of[HEADS * t: HEADS * (t + 1), :], axis=0, keepdims=True)
            ob_ref[0, t:t + 1, :] = jnp.sum(os_[HEADS * t: HEADS * (t + 1), :], axis=0, keepdims=True)


def _decode_attention(layer, page_table, pool_fk, pool_fv, pool_lft, pool_sk, pool_sv,
                      qhat, khat, lfn, u3):
    b, n_pages = page_table.shape
    n_new = qhat.shape[1]
    npg = PAGES_PER_STEP
    nsteps = n_pages // npg
    rows = HEADS * n_new

    def page_spec(shape, i):
        def imap(bi, c, pt, i=i):
            return (layer, pt[bi, n_pages - 1 - (c * npg + i)], 0, 0)
        return pl.BlockSpec(shape, imap)

    def per_batch(w, col=0):
        return pl.BlockSpec((1, n_new, w), lambda bi, c, pt, col=col, w=w: (bi, 0, col // w))

    in_specs = ([page_spec((1, 1, AW, PAGE), i) for i in range(npg)]
                + [page_spec((1, 1, AW, PAGE), i) for i in range(npg)]
                + [page_spec((1, 1, HEADS, PAGE), i) for i in range(npg)]
                + [page_spec((1, 1, AW, PAGE), i) for i in range(npg)]
                + [page_spec((1, 1, AW, PAGE), i) for i in range(npg)]
                + [per_batch(AW), per_batch(AW), per_batch(LANES),
                   per_batch(AW, C_FV), per_batch(AW, C_SQ), per_batch(AW, C_SK), per_batch(AW, C_SV),
                   pl.BlockSpec((PAGE, PAGE), lambda bi, c, pt: (0, 0))])
    grid_spec = pltpu.PrefetchScalarGridSpec(
        num_scalar_prefetch=1,
        grid=(b, nsteps),
        in_specs=in_specs,
        out_specs=[pl.BlockSpec((1, n_new, AW), lambda bi, c, pt: (bi, 0, 0)),
                   pl.BlockSpec((1, n_new, AW), lambda bi, c, pt: (bi, 0, 0))],
        scratch_shapes=[pltpu.VMEM((rows, AW), BF16), pltpu.VMEM((rows, AW), BF16),
                        pltpu.VMEM((rows, 1), F32), pltpu.VMEM((rows, 1), F32),
                        pltpu.VMEM((rows, 1), F32), pltpu.VMEM((rows, 1), F32),
                        pltpu.VMEM((rows, AW), F32), pltpu.VMEM((rows, AW), F32)],
    )
    mst = jnp.asarray(np.tril(np.ones((PAGE, PAGE), np.float32), -1), BF16)
    args = ([pool_fk] * npg + [pool_fv] * npg + [pool_lft] * npg + [pool_sk] * npg + [pool_sv] * npg
            + [qhat, khat, lfn, u3, u3, u3, u3, mst])
    return pl.pallas_call(
        functools.partial(_decode_kernel, layer=layer, n_new=n_new),
        grid_spec=grid_spec,
        out_shape=[jax.ShapeDtypeStruct((b, n_new, AW), F32), jax.ShapeDtypeStruct((b, n_new, AW), F32)],
        compiler_params=_cparams(("arbitrary", "arbitrary")),
        name="decode_attention",
    )(page_table, *args)


def _ssd_kernel(*refs, t_valid, has_init):
    if has_init:
        (xbc_ref, z_ref, sm_ref, h0_ref, cprev_ref, cw_ref, cb_ref, dtb_ref, alog_ref, dsk_ref, nw_ref,
         ex_ref, tri_ref, y_ref, hl_ref, ext_ref, ht_ref) = refs
    else:
        (xbc_ref, z_ref, sm_ref, cw_ref, cb_ref, dtb_ref, alog_ref, dsk_ref, nw_ref,
         ex_ref, tri_ref, y_ref, hl_ref, ext_ref, ht_ref) = refs
    T = SSM_CHUNK
    GW = SSM_W // SSM_GROUPS
    c = pl.program_id(1)

    @pl.when(c == 0)
    def _():
        if has_init:
            for g in range(SSM_GROUPS):
                ht_ref[g] = h0_ref[0, GW * g: GW * (g + 1), :].T
            ext_ref[0:SUBLANES, :] = cprev_ref[0]
        else:
            ht_ref[...] = jnp.zeros_like(ht_ref)
            ext_ref[0:SUBLANES, :] = jnp.zeros((SUBLANES, SSM_CONV_DIM), F32)

    if t_valid == T:
        ext_ref[SUBLANES:SUBLANES + T, :] = xbc_ref[...]
    else:
        ext_ref[SUBLANES:SUBLANES + T, :] = jnp.zeros((T, SSM_CONV_DIM), F32)
        ext_ref[SUBLANES:SUBLANES + t_valid, :] = xbc_ref[0]

    conv = cb_ref[...]
    for j in range(SSM_CONV):
        off = SUBLANES - (SSM_CONV - 1) + j
        conv = conv + ext_ref[off: off + T, :] * cw_ref[j:j + 1, :]
    act = _silu(conv)
    xs = act[:, :SSM_W]
    bm = act[:, SSM_W: SSM_W + SSM_GROUPS * SSM_STATE]
    cm = act[:, SSM_W + SSM_GROUPS * SSM_STATE:]

    sm = sm_ref[...] if t_valid == T else jnp.concatenate(
        [sm_ref[0], jnp.zeros((T - t_valid, LANES), F32)], axis=0)
    lane = lax.broadcasted_iota(jnp.int32, (T, LANES), 1)
    row = lax.broadcasted_iota(jnp.int32, (T, LANES), 0)
    live = (lane >= DT_LANE0) & (lane < DT_LANE0 + SSM_HEADS) & (row < t_valid)
    dt = jnp.where(live, _softplus(sm + dtb_ref[...]), 0.0)
    a = -jnp.exp(alog_ref[...])
    da = dt * a
    tri = tri_ref[...]
    p3 = _split3(da)
    acum = _dot(tri, p3[0]) + _dot(tri, p3[1]) + _dot(tri, p3[2])

    ex = ex_ref[...]
    acum_e = _dot3(_split3(acum), ex)
    dt_e = _dot3(_split3(dt), ex)
    total_e = acum_e[T - 1:T, :]
    exp_acum_e = jnp.exp(acum_e)
    wend_e = jnp.exp(total_e - acum_e) * dt_e
    cdecay_e = jnp.exp(total_e)

    xw = (xs * wend_e).astype(BF16)
    acum_t = acum.T
    dt_t = dt.T
    rr = lax.broadcasted_iota(jnp.int32, (T, T), 0)
    cc = lax.broadcasted_iota(jnp.int32, (T, T), 1)
    causal = rr >= cc
    xs_b = xs.astype(BF16)
    zero_b = jnp.zeros((T, LANES), BF16)

    y_parts = []
    for g in range(SSM_GROUPS):
        bg = bm[:, SSM_STATE * g: SSM_STATE * (g + 1)]
        cg = cm[:, SSM_STATE * g: SSM_STATE * (g + 1)].astype(BF16)
        cb = _dot_nt(cg, bg.astype(BF16))
        h_prev = ht_ref[g]
        yoff = _dot(cg, h_prev.astype(BF16)) * exp_acum_e[:, GW * g: GW * (g + 1)]
        st = _dot(bg.T.astype(BF16), xw[:, GW * g: GW * (g + 1)])
        ht_ref[g] = h_prev * cdecay_e[:, GW * g: GW * (g + 1)] + st
        hpg = SSM_HEADS // SSM_GROUPS
        for jp in range(hpg // 2):
            pair = (hpg // 2) * g + jp
            xp = xs_b[:, LANES * pair: LANES * (pair + 1)]
            yp = jnp.zeros((T, LANES), F32)
            for e in range(2):
                h = 2 * pair + e
                ln = DT_LANE0 + h
                seg = acum[:, ln:ln + 1] - acum_t[ln:ln + 1, :]
                dec = jnp.exp(jnp.where(causal, seg, NEG_INF))
                w = cb * dec * dt_t[ln:ln + 1, :]
                xh = jnp.where(_half_mask((T, LANES), e), xp, zero_b)
                yp = yp + _dot(w.astype(BF16), xh)
            y_parts.append(yp + yoff[:, LANES * jp: LANES * (jp + 1)])
    y = jnp.concatenate(y_parts, axis=1)
    y = y + xs * dsk_ref[...]
    zz = z_ref[...] if t_valid == T else jnp.concatenate(
        [z_ref[0], jnp.zeros((T - t_valid, SSM_W), F32)], axis=0)
    y = y * _silu(zz)
    outs = []
    for g in range(SSM_GROUPS):
        yg = y[:, GW * g: GW * (g + 1)]
        ms = jnp.mean(yg * yg, axis=-1, keepdims=True)
        outs.append(yg * lax.rsqrt(ms + NORM_EPS) * nw_ref[:, GW * g: GW * (g + 1)])
    yo = jnp.concatenate(outs, axis=1)
    if t_valid == T:
        y_ref[...] = yo
    else:
        y_ref[0] = yo[:t_valid, :]

    ext_ref[0:SUBLANES, :] = ext_ref[T: T + SUBLANES, :]

    @pl.when(c == pl.num_programs(1) - 1)
    def _():
        for g in range(SSM_GROUPS):
            hl_ref[0, GW * g: GW * (g + 1), :] = ht_ref[g].T


def _ssd_consts():
    ex = np.zeros((LANES, SSM_W), np.float32)
    for h in range(SSM_HEADS):
        ex[DT_LANE0 + h, SSM_HD * h: SSM_HD * (h + 1)] = 1.0
    tri = np.tril(np.ones((SSM_CHUNK, SSM_CHUNK), np.float32))
    return jnp.asarray(ex, BF16), jnp.asarray(tri, BF16)


def _ssd(u, b, s, params, h0=None, cprev=None):
    cw, cb, dtb, alog, dsk, nw = params
    ex, tri = _ssd_consts()
    T = SSM_CHUNK
    has_init = h0 is not None
    prompt = (s % T == 0)
    nc = s // T if prompt else 1
    t_valid = T if prompt else s

    def const(shape):
        return pl.BlockSpec(shape, lambda bi, c: (0,) * len(shape))

    if prompt:
        def ucol(col, w):
            return pl.BlockSpec((T, w), lambda bi, c, col=col, w=w: (bi * nc + c, col // w))
        y_spec = pl.BlockSpec((T, SSM_W), lambda bi, c: (bi * nc + c, 0))
        y_shape = jax.ShapeDtypeStruct((b * s, SSM_W), F32)
    else:
        def ucol(col, w):
            return pl.BlockSpec((1, s, w), lambda bi, c, col=col, w=w: (bi, 0, col // w))
        y_spec = pl.BlockSpec((1, s, SSM_W), lambda bi, c: (bi, 0, 0))
        y_shape = jax.ShapeDtypeStruct((b, s, SSM_W), F32)
    in_specs = [ucol(C_XBC, SSM_CONV_DIM), ucol(C_Z, SSM_W), ucol(C_SMALL, LANES)]
    args = [u, u, u]
    if has_init:
        in_specs += [pl.BlockSpec((1, SSM_W, SSM_STATE), lambda bi, c: (bi, 0, 0)),
                     pl.BlockSpec((1, SUBLANES, SSM_CONV_DIM), lambda bi, c: (bi, 0, 0))]
        args += [h0, cprev]
    in_specs += [const((SUBLANES, SSM_CONV_DIM)), const((1, SSM_CONV_DIM)), const((1, LANES)),
                 const((1, LANES)), const((1, SSM_W)), const((1, SSM_W)),
                 const((LANES, SSM_W)), const((T, T))]
    args += [cw, cb, dtb, alog, dsk, nw, ex, tri]
    return pl.pallas_call(
        functools.partial(_ssd_kernel, t_valid=t_valid, has_init=has_init),
        grid=(b, nc),
        in_specs=in_specs,
        out_specs=[y_spec, pl.BlockSpec((1, SSM_W, SSM_STATE), lambda bi, c: (bi, 0, 0))],
        out_shape=[y_shape, jax.ShapeDtypeStruct((b, SSM_W, SSM_STATE), F32)],
        scratch_shapes=[pltpu.VMEM((T + 2 * SUBLANES, SSM_CONV_DIM), F32),
                        pltpu.VMEM((SSM_GROUPS, SSM_STATE, SSM_W // SSM_GROUPS), F32)],
        compiler_params=_cparams(("arbitrary", "arbitrary")),
        name="ssd",
    )(*args)


def _merge_kernel(x_ref, oa_ref, fg_ref, ob_ref, sg_ref, yc_ref, ga_ref, gb_ref, gc_ref, bg_ref,
                  wpa_ref, wpb_ref, wpc_ref, wout_ref, out_ref):
    ya = (oa_ref[...] * _silu(fg_ref[...])).astype(BF16)
    yb = (ob_ref[...] * _silu(sg_ref[...])).astype(BF16)
    yc = yc_ref[...].astype(BF16)
    h = _sigmoid(ga_ref[...] + bg_ref[:, 0:D_MODEL]) * _dot(ya, wpa_ref[...])
    h = h + _sigmoid(gb_ref[...] + bg_ref[:, D_MODEL:2 * D_MODEL]) * _dot(yb, wpb_ref[...])
    h = h + _sigmoid(gc_ref[...] + bg_ref[:, 2 * D_MODEL:3 * D_MODEL]) * _dot(yc, wpc_ref[...])
    out_ref[...] = x_ref[...] + _dot(h.astype(BF16), wout_ref[...])


def _merge(x2d, u, oa, ob, yc, bg, wpa, wpb, wpc, wout, tm):
    n = x2d.shape[0]

    def rows(w, col=0):
        return pl.BlockSpec((tm, w), lambda i, col=col, w=w: (i, col // w))

    def const(shape):
        return pl.BlockSpec(shape, lambda i: (0,) * len(shape))

    return pl.pallas_call(
        _merge_kernel,
        grid=(n // tm,),
        in_specs=[rows(D_MODEL), rows(AW), rows(AW, C_FG), rows(AW), rows(AW, C_SG), rows(SSM_W),
                  rows(D_MODEL, C_GL), rows(D_MODEL, C_GL + D_MODEL), rows(D_MODEL, C_GL + 2 * D_MODEL),
                  const((1, 3 * D_MODEL)), const((AW, D_MODEL)), const((AW, D_MODEL)),
                  const((SSM_W, D_MODEL)), const((D_MODEL, D_MODEL))],
        out_specs=rows(D_MODEL),
        out_shape=jax.ShapeDtypeStruct((n, D_MODEL), F32),
        compiler_params=_cparams(("arbitrary",)),
        name="merge",
    )(x2d, oa, u, ob, u, yc, u, u, u, bg, wpa, wpb, wpc, wout)


def _regroup_w_in(w):
    d = w.shape[0]
    z = lambda n: jnp.zeros((d, n), w.dtype)
    fox = jnp.concatenate([w[:, 0:1536], w[:, 1544:2056]], axis=1)
    sb = w[:, 2056:4104]
    cz = w[:, 4104:5128]
    cxbc = w[:, 5128:6664]
    cdt = w[:, 6664:6680]
    gl = w[:, 6680:9752]
    ff = w[:, 1536:1544]
    small = jnp.concatenate([ff, cdt, z(C_XBC - C_SMALL - 24)], axis=1)
    return jnp.concatenate([fox, sb, small, cxbc, cz, gl], axis=1).astype(BF16)


def _pad_lanes(v, lane0, width=LANES):
    out = jnp.zeros((1, width), F32)
    return out.at[0, lane0: lane0 + v.shape[0]].set(v.astype(F32))


def kernel(x_prompt, x_sample, cache_fox_k, cache_fox_v, cache_fox_logf, cache_sb_k, cache_sb_v,
           state_ssm, state_conv, page_table, norm_g, w_in, b_forget, fox_q_norm, fox_k_norm,
           conv_w, conv_b, dt_bias, a_log, d_skip, ssm_norm, b_gate, w_pa, w_pb, w_pc, w_out):
    depth = norm_g.shape[0]
    bp, sp, _ = x_prompt.shape
    bs, ss, _ = x_sample.shape
    n_pool = cache_fox_k.shape[1]
    npr, nsm = bp * sp, bs * ss

    def pool_view(c):
        return jnp.transpose(c, (0, 1, 3, 4, 2)).reshape(depth, n_pool, AW, PAGE)

    def cache_leaf(t):
        return jnp.transpose(t.reshape(t.shape[0], HEADS, HD, t.shape[2]), (0, 3, 1, 2))

    pool_fk, pool_fv, pool_sk, pool_sv = (pool_view(c) for c in
                                          (cache_fox_k, cache_fox_v, cache_sb_k, cache_sb_v))
    pool_lft = jnp.swapaxes(cache_fox_logf, 2, 3)
    h0_all = state_ssm.reshape(depth, bs, SSM_W, SSM_STATE)
    cprev_all = jnp.pad(state_conv, ((0, 0), (0, 0), (SUBLANES - (SSM_CONV - 1), 0), (0, 0)))

    xp = x_prompt.reshape(npr, D_MODEL)
    xs = x_sample.reshape(nsm, D_MODEL)
    new_p, new_s = [], []
    for l in range(depth):
        w_pad = _regroup_w_in(w_in[l])
        g = norm_g[l].reshape(1, D_MODEL)
        gq = jnp.tile(fox_q_norm[l], HEADS).reshape(1, AW) * (HD ** -0.5)
        gk = jnp.tile(fox_k_norm[l], HEADS).reshape(1, AW)
        bf = _pad_lanes(b_forget[l], 0)
        ssd_params = (jnp.pad(conv_w[l], ((0, SUBLANES - SSM_CONV), (0, 0))), conv_b[l].reshape(1, -1),
                      _pad_lanes(dt_bias[l], DT_LANE0), _pad_lanes(a_log[l], DT_LANE0),
                      jnp.repeat(d_skip[l], SSM_HD).reshape(1, SSM_W), ssm_norm[l].reshape(1, SSM_W))
        bg = b_gate[l].reshape(1, -1)
        wpa, wpb, wpc, wout = (w_pa[l].astype(BF16), w_pb[l].astype(BF16), w_pc[l].astype(BF16),
                               w_out[l].astype(BF16))

        u = _in_proj(xp, g, w_pad, tm=min(1024, npr))
        kt, vt, skt, svt, lft, qa, ka, vtf, qs, ks, vts = _prep_prompt(u, bp, sp, gq, gk, bf)
        oa = _fox_attention(qa, ka, vtf).reshape(npr, AW)
        ob = _sb_attention(qs, ks, vts).reshape(npr, AW)
        yc, h_last = _ssd(u, bp, sp, ssd_params)
        u3 = u.reshape(bp, sp, U_W)
        new_p.append((cache_leaf(kt), cache_leaf(vt), jnp.transpose(lft, (0, 2, 1)),
                      cache_leaf(skt), cache_leaf(svt),
                      h_last.reshape(bp, SSM_HEADS, SSM_HD, SSM_STATE),
                      u3[:, sp - (SSM_CONV - 1):, C_XBC:C_XBC + SSM_CONV_DIM]))
        xp = _merge(xp, u, oa, ob, yc, bg, wpa, wpb, wpc, wout, tm=min(256, npr))

        us = _in_proj(xs, g, w_pad, tm=nsm)
        qhat_s, khat_s, logf_s = _prep_sample(us, gq, gk, bf)
        us3 = us.reshape(bs, ss, U_W)
        oa_s, ob_s = _decode_attention(l, page_table, pool_fk, pool_fv, pool_lft, pool_sk, pool_sv,
                                       qhat_s.reshape(bs, ss, AW), khat_s.reshape(bs, ss, AW),
                                       logf_s.reshape(bs, ss, LANES), us3)
        yc_s, h_last_s = _ssd(us3, bs, ss, ssd_params, h0=h0_all[l], cprev=cprev_all[l])
        conv_cat = jnp.concatenate([state_conv[l], us3[:, :, C_XBC:C_XBC + SSM_CONV_DIM]], axis=1)
        new_s.append((khat_s.reshape(bs, ss, HEADS, HD),
                      us3[:, :, C_FV:C_FV + AW].reshape(bs, ss, HEADS, HD),
                      logf_s[:, :HEADS].reshape(bs, ss, HEADS),
                      us3[:, :, C_SK:C_SK + AW].reshape(bs, ss, HEADS, HD),
                      us3[:, :, C_SV:C_SV + AW].reshape(bs, ss, HEADS, HD),
                      h_last_s.reshape(bs, SSM_HEADS, SSM_HD, SSM_STATE),
                      conv_cat[:, -(SSM_CONV - 1):]))
        xs = _merge(xs, us, oa_s.reshape(nsm, AW), ob_s.reshape(nsm, AW), yc_s.reshape(nsm, SSM_W),
                    bg, wpa, wpb, wpc, wout, tm=nsm)

    outs_p = [jnp.stack([st[i] for st in new_p]) for i in range(7)]
    outs_s = [jnp.stack([st[i] for st in new_s]) for i in range(7)]
    return (xp.reshape(bp, sp, D_MODEL), xs.reshape(bs, ss, D_MODEL), *outs_p, *outs_s)
```

```python
import functools

import numpy as np
import jax
import jax.numpy as jnp
from jax import lax
from jax.experimental import pallas as pl
from jax.experimental.pallas import tpu as pltpu

F32 = jnp.float32
BF16 = jnp.bfloat16

D_MODEL = 1024
HEADS = 8
HD = 64
AW = HEADS * HD
SSM_W = 1024
SSM_HEADS = 16
SSM_HD = 64
SSM_GROUPS = 2
SSM_STATE = 128
SSM_CONV = 4
SSM_CONV_DIM = SSM_W + 2 * SSM_GROUPS * SSM_STATE
SSM_CHUNK = 128
PAGE = 128
NORM_EPS = 1e-6
NEG_INF = float("-inf")

LANES = 128
SUBLANES = 8
VMEM_LIMIT = 56 * 1024 * 1024

C_FQ, C_FK, C_FV, C_FG = 0, 512, 1024, 1536
C_SQ, C_SK, C_SV, C_SG = 2048, 2560, 3072, 3584
C_SMALL = 4096
C_XBC = 4608
C_Z = 6144
C_GL = 7168
U_W = 10240
DT_LANE0 = 8
PAGES_PER_STEP = 8

FOX_TQ = 1024
FOX_TK = 512
SB_TQ = 256
SB_TK = 256
LOG2E = 1.4426950408889634
SB_DEAD_LOG2 = -151.0
SB_DEAD_LN = -104.0


def _cparams(sem):
    return pltpu.CompilerParams(dimension_semantics=sem, vmem_limit_bytes=VMEM_LIMIT)


def _logsig(x):
    return jnp.minimum(x, 0.0) - jnp.log1p(jnp.exp(-jnp.abs(x)))


def _softplus(x):
    return jnp.maximum(x, 0.0) + jnp.log1p(jnp.exp(-jnp.abs(x)))


def _sigmoid(x):
    return 1.0 / (1.0 + jnp.exp(-x))


def _silu(x):
    return x * _sigmoid(x)


def _split3(x):
    hi = x.astype(BF16)
    r1 = x - hi.astype(F32)
    mid = r1.astype(BF16)
    lo = (r1 - mid.astype(F32)).astype(BF16)
    return hi, mid, lo


def _split2(x):
    hi = x.astype(BF16)
    lo = (x - hi.astype(F32)).astype(BF16)
    return hi, lo


def _dot(a, b):
    return jnp.dot(a, b, preferred_element_type=F32)


def _dot_nt(a, b):
    return lax.dot_general(a, b, (((1,), (1,)), ((), ())), preferred_element_type=F32)


def _dot3(pieces, m):
    out = _dot(pieces[0], m)
    for p in pieces[1:]:
        out = out + _dot(p, m)
    return out


def _in_proj_kernel(x_ref, g_ref, w_ref, u_ref, xn_ref):
    @pl.when(pl.program_id(1) == 0)
    def _():
        x = x_ref[...]
        ms = jnp.mean(x * x, axis=-1, keepdims=True)
        xn_ref[...] = (x * lax.rsqrt(ms + NORM_EPS) * g_ref[...]).astype(BF16)

    u_ref[...] = _dot(xn_ref[...], w_ref[...])


def _in_proj(x2d, g, w_pad, tm, tn=2048):
    n = x2d.shape[0]
    return pl.pallas_call(
        _in_proj_kernel,
        grid=(n // tm, U_W // tn),
        in_specs=[
            pl.BlockSpec((tm, D_MODEL), lambda i, j: (i, 0)),
            pl.BlockSpec((1, D_MODEL), lambda i, j: (0, 0)),
            pl.BlockSpec((D_MODEL, tn), lambda i, j: (0, j)),
        ],
        out_specs=pl.BlockSpec((tm, tn), lambda i, j: (i, j)),
        out_shape=jax.ShapeDtypeStruct((n, U_W), F32),
        scratch_shapes=[pltpu.VMEM((tm, D_MODEL), BF16)],
        compiler_params=_cparams(("arbitrary", "arbitrary")),
        name="in_proj",
    )(x2d, g, w_pad)


def _head_norm(x, gain, bd):
    xx = x * x
    hi, lo = _split2(xx)
    ss = _dot(hi, bd) + _dot(lo, bd)
    return x * lax.rsqrt(ss * (1.0 / HD) + NORM_EPS) * gain


def _half_mask(shape, head):
    lane = lax.broadcasted_iota(jnp.int32, shape, len(shape) - 1)
    return (lane >= HD) if (head % 2) else (lane < HD)


def _prep_prompt_kernel(fq_ref, fk_ref, fv_ref, sq_ref, sk_ref, sv_ref, sm_ref,
                        gq_ref, gk_ref, bf_ref, bd_ref, tri_ref, sqm_ref, skm_ref, oq_ref, ok_ref,
                        kt_ref, vt_ref, skt_ref, svt_ref, lft_ref, qa_ref, ka_ref, vtf_ref, qs_ref, ks_ref,
                        vts_ref, carry_ref):
    @pl.when(pl.program_id(1) == 0)
    def _():
        carry_ref[...] = jnp.zeros_like(carry_ref)

    bd = bd_ref[...]
    q = _head_norm(fq_ref[...], gq_ref[...], bd)
    k = _head_norm(fk_ref[...], gk_ref[...], bd)

    sm = sm_ref[...]
    lane = lax.broadcasted_iota(jnp.int32, sm.shape, 1)
    lf = jnp.where(lane < HEADS, _logsig(sm + bf_ref[...]), 0.0)
    lft_ref[0] = lf.T[0:HEADS, :]

    tri = tri_ref[...]
    lf3 = _split3(lf)
    cum = _dot(tri, lf3[0]) + _dot(tri, lf3[1]) + _dot(tri, lf3[2]) + carry_ref[...]
    tm = cum.shape[0]
    carry_ref[...] = cum[tm - 1:tm, :]

    cat = jnp.concatenate(_split3(cum * LOG2E), axis=1)
    aug_q = (_dot(cat, sqm_ref[...]) + oq_ref[...]).astype(BF16)
    aug_k = (_dot(cat, skm_ref[...]) + ok_ref[...]).astype(BF16)

    qb = q.astype(BF16)
    kb = k.astype(BF16)
    fv = fv_ref[...]
    sq = (sq_ref[...] * (HD ** -0.5 * LOG2E)).astype(BF16)
    sk32 = sk_ref[...]
    sk = sk32.astype(BF16)
    sv = sv_ref[...]
    qs_ref[0] = sq
    zero = jnp.zeros((tm, LANES), BF16)
    for j in range(HEADS // 2):
        ls = slice(LANES * j, LANES * (j + 1))
        qa_ref[0, :, 2 * LANES * j: 2 * LANES * j + LANES] = qb[:, ls]
        qa_ref[0, :, 2 * LANES * j + LANES: 2 * LANES * (j + 1)] = aug_q[:, ls]
        fvt = fv[:, ls].T
        svt = sv[:, ls].T
        vt_ref[0, ls, :] = fvt
        svt_ref[0, ls, :] = svt
        for c in range(tm // FOX_TK):
            vtf_ref[0, j, c] = fvt[:, FOX_TK * c: FOX_TK * (c + 1)].astype(BF16)
        for c in range(tm // SB_TK):
            vts_ref[0, j, c] = svt[:, SB_TK * c: SB_TK * (c + 1)].astype(BF16)
        kt_ref[0, ls, :] = k[:, ls].T
        skt_ref[0, ls, :] = sk32[:, ls].T
        for e in range(2):
            h = 2 * j + e
            hm = _half_mask((tm, LANES), h)
            ka_ref[0, :, 2 * LANES * h: 2 * LANES * h + LANES] = jnp.where(hm, kb[:, ls], zero)
            ka_ref[0, :, 2 * LANES * h + LANES: 2 * LANES * (h + 1)] = aug_k[:, LANES * h: LANES * (h + 1)]
            ks_ref[0, :, LANES * h: LANES * (h + 1)] = jnp.where(hm, sk[:, ls], zero)


def _prep_consts(tm):
    bd = np.kron(np.eye(HEADS, dtype=np.float32), np.ones((HD, HD), np.float32))
    tri = np.tril(np.ones((tm, tm), np.float32))
    sqm = np.zeros((3 * LANES, AW), np.float32)
    skm = np.zeros((3 * LANES, 2 * AW), np.float32)
    oq = np.zeros((1, AW), np.float32)
    ok = np.zeros((1, 2 * AW), np.float32)
    for h in range(HEADS):
        base = (h % 2) * 6
        for i in range(3):
            sqm[i * LANES + h, (h // 2) * LANES + base + i] = 1.0
            skm[i * LANES + h, h * LANES + base + 3 + i] = -1.0
            oq[0, (h // 2) * LANES + base + 3 + i] = 1.0
            ok[0, h * LANES + base + i] = 1.0
    return (jnp.asarray(bd, BF16), jnp.asarray(tri, BF16), jnp.asarray(sqm, BF16),
            jnp.asarray(skm, BF16), jnp.asarray(oq), jnp.asarray(ok))


def _prep_prompt(u, b, s, gq, gk, bf, tm=512):
    nt = s // tm
    bd, tri, sqm, skm, oq, ok = _prep_consts(tm)

    def ucol(c, w):
        return pl.BlockSpec((tm, w), lambda bi, i, c=c, w=w: (bi * nt + i, c // w))

    def const(shape):
        return pl.BlockSpec(shape, lambda bi, i: (0,) * len(shape))

    outs = pl.pallas_call(
        _prep_prompt_kernel,
        grid=(b, nt),
        in_specs=[ucol(C_FQ, AW), ucol(C_FK, AW), ucol(C_FV, AW), ucol(C_SQ, AW), ucol(C_SK, AW),
                  ucol(C_SV, AW), ucol(C_SMALL, LANES),
                  const((1, AW)), const((1, AW)), const((1, LANES)), const((AW, AW)), const((tm, tm)),
                  const((3 * LANES, AW)), const((3 * LANES, 2 * AW)), const((1, AW)), const((1, 2 * AW))],
        out_specs=[
            pl.BlockSpec((1, AW, tm), lambda bi, i: (bi, 0, i)),
            pl.BlockSpec((1, AW, tm), lambda bi, i: (bi, 0, i)),
            pl.BlockSpec((1, AW, tm), lambda bi, i: (bi, 0, i)),
            pl.BlockSpec((1, AW, tm), lambda bi, i: (bi, 0, i)),
            pl.BlockSpec((1, HEADS, tm), lambda bi, i: (bi, 0, i)),
            pl.BlockSpec((1, tm, 2 * AW), lambda bi, i: (bi, i, 0)),
            pl.BlockSpec((1, tm, 4 * AW), lambda bi, i: (bi, i, 0)),
            pl.BlockSpec((1, HEADS // 2, tm // FOX_TK, LANES, FOX_TK), lambda bi, i: (bi, 0, i, 0, 0)),
            pl.BlockSpec((1, tm, AW), lambda bi, i: (bi, i, 0)),
            pl.BlockSpec((1, tm, 2 * AW), lambda bi, i: (bi, i, 0)),
            pl.BlockSpec((1, HEADS // 2, tm // SB_TK, LANES, SB_TK), lambda bi, i: (bi, 0, i, 0, 0)),
        ],
        out_shape=[
            jax.ShapeDtypeStruct((b, AW, s), F32),
            jax.ShapeDtypeStruct((b, AW, s), F32),
            jax.ShapeDtypeStruct((b, AW, s), F32),
            jax.ShapeDtypeStruct((b, AW, s), F32),
            jax.ShapeDtypeStruct((b, HEADS, s), F32),
            jax.ShapeDtypeStruct((b, s, 2 * AW), BF16),
            jax.ShapeDtypeStruct((b, s, 4 * AW), BF16),
            jax.ShapeDtypeStruct((b, HEADS // 2, s // FOX_TK, LANES, FOX_TK), BF16),
            jax.ShapeDtypeStruct((b, s, AW), BF16),
            jax.ShapeDtypeStruct((b, s, 2 * AW), BF16),
            jax.ShapeDtypeStruct((b, HEADS // 2, s // SB_TK, LANES, SB_TK), BF16),
        ],
        scratch_shapes=[pltpu.VMEM((1, LANES), F32)],
        compiler_params=_cparams(("arbitrary", "arbitrary")),
        name="prep_prompt",
    )(u, u, u, u, u, u, u, gq, gk, bf, bd, tri, sqm, skm, oq, ok)
    return outs


def _prep_sample_kernel(fq_ref, fk_ref, sm_ref, gq_ref, gk_ref, bf_ref, bd_ref,
                        qhat_ref, khat_ref, logf_ref):
    bd = bd_ref[...]
    qhat_ref[...] = _head_norm(fq_ref[...], gq_ref[...], bd)
    khat_ref[...] = _head_norm(fk_ref[...], gk_ref[...], bd)
    sm = sm_ref[...]
    lane = lax.broadcasted_iota(jnp.int32, sm.shape, 1)
    logf_ref[...] = jnp.where(lane < HEADS, _logsig(sm + bf_ref[...]), 0.0)


def _prep_sample(u, gq, gk, bf):
    n = u.shape[0]
    bd = _prep_consts(8)[0]

    def ucol(c, w):
        return pl.BlockSpec((n, w), lambda i, c=c, w=w: (0, c // w))

    def const(shape):
        return pl.BlockSpec(shape, lambda i: (0,) * len(shape))

    return pl.pallas_call(
        _prep_sample_kernel,
        grid=(1,),
        in_specs=[ucol(C_FQ, AW), ucol(C_FK, AW), ucol(C_SMALL, LANES),
                  const((1, AW)), const((1, AW)), const((1, LANES)), const((AW, AW))],
        out_specs=[pl.BlockSpec((n, AW), lambda i: (0, 0)), pl.BlockSpec((n, AW), lambda i: (0, 0)),
                   pl.BlockSpec((n, LANES), lambda i: (0, 0))],
        out_shape=[jax.ShapeDtypeStruct((n, AW), F32), jax.ShapeDtypeStruct((n, AW), F32),
                   jax.ShapeDtypeStruct((n, LANES), F32)],
        compiler_params=_cparams(("arbitrary",)),
        name="prep_sample",
    )(u, u, u, gq, gk, bf, bd)


def _fox_kernel(qa_ref, ka_ref, vt_ref, o_ref, m_ref, l_ref, acc_ref):
    qi = pl.program_id(2)
    tq, tk = FOX_TQ, FOX_TK
    m_ref[...] = jnp.full_like(m_ref, NEG_INF)
    l_ref[...] = jnp.zeros_like(l_ref)
    acc_ref[...] = jnp.zeros_like(acc_ref)
    q = qa_ref[0]

    def block(ki, masked):
        k0 = pl.multiple_of(ki * tk, tk)
        for e in range(2):
            k = ka_ref[0, pl.ds(k0, tk), 2 * LANES * e: 2 * LANES * (e + 1)]
            st = _dot_nt(k, q)
            if masked is not None:
                r = lax.broadcasted_iota(jnp.int32, st.shape, 0) + tk * masked
                c = lax.broadcasted_iota(jnp.int32, st.shape, 1)
                st = jnp.where(r <= c, st, NEG_INF)
            m_prev = m_ref[e:e + 1, :]
            m_new = jnp.maximum(m_prev, jnp.max(st, axis=0, keepdims=True))
            alpha = jnp.exp2(m_prev - m_new)
            p = jnp.exp2(st - m_new)
            l_ref[e:e + 1, :] = alpha * l_ref[e:e + 1, :] + jnp.sum(p, axis=0, keepdims=True)
            m_ref[e:e + 1, :] = m_new
            pv = _dot(vt_ref[0, 0, ki, HD * e: HD * (e + 1), :], p.astype(BF16))
            acc_ref[HD * e: HD * (e + 1), :] = alpha * acc_ref[HD * e: HD * (e + 1), :] + pv

    def loop_body(ki, carry):
        block(ki, None)
        return carry

    nd = tq // tk
    lax.fori_loop(0, qi * nd, loop_body, 0)
    for d in range(nd):
        block(qi * nd + d, d)
    o0 = acc_ref[0:HD, :] / l_ref[0:1, :]
    o1 = acc_ref[HD:2 * HD, :] / l_ref[1:2, :]
    o_ref[0] = jnp.concatenate([o0, o1], axis=0).T


def _sb_kernel(mt2_ref, q_ref, k_ref, vt_ref, o_ref, carry_ref, acc_ref):
    qi = pl.program_id(2)
    tq, tk = SB_TQ, SB_TK
    nsub = tk // LANES
    carry_ref[...] = jnp.zeros_like(carry_ref)
    acc_ref[...] = jnp.zeros_like(acc_ref)
    mt2 = mt2_ref[...]
    q = q_ref[0]

    def block(ki, masked):
        k0 = pl.multiple_of(ki * tk, tk)
        for e in range(2):
            k = k_ref[0, pl.ds(k0, tk), LANES * e: LANES * (e + 1)]
            zt = _dot_nt(k, q)
            carry = carry_ref[e:e + 1, :]
            a_parts = [None] * nsub
            for sub in range(nsub - 1, -1, -1):
                z = zt[LANES * sub: LANES * (sub + 1), :]
                lp = jnp.log2(1.0 + jnp.exp2(-jnp.abs(z)))
                lsig = jnp.minimum(z, 0.0) - lp
                lskip = lsig - z
                if masked:
                    r = lax.broadcasted_iota(jnp.int32, z.shape, 0) + LANES * sub
                    c = lax.broadcasted_iota(jnp.int32, z.shape, 1)
                    valid = r < c
                    lskip = jnp.where(valid, lskip, 0.0)
                hi, lo = _split2(lskip)
                after = _dot(mt2, jnp.concatenate([hi, lo], axis=0)) + carry
                a = jnp.exp2(lsig + after)
                if masked:
                    a = jnp.where(valid, a, 0.0)
                a_parts[sub] = a.astype(BF16)
                carry = carry + jnp.sum(lskip, axis=0, keepdims=True)
            carry_ref[e:e + 1, :] = carry
            av = _dot(vt_ref[0, 0, ki, HD * e: HD * (e + 1), :], jnp.concatenate(a_parts, axis=0))
            acc_ref[HD * e: HD * (e + 1), :] = acc_ref[HD * e: HD * (e + 1), :] + av

    block(qi, True)
    carry_ref[...] = jnp.where(qi > 0, carry_ref[...], -1e30)
    block(jnp.maximum(qi - 1, 0), False)

    def cond(ki):
        return jnp.logical_and(ki >= 0, jnp.max(carry_ref[0:2, :]) > SB_DEAD_LOG2)

    def loop_body(ki):
        block(ki, False)
        return ki - 1

    lax.while_loop(cond, loop_body, qi - 2)
    o_ref[0] = acc_ref[...].T


def _fox_attention(qa, ka, vt):
    b, s, _ = qa.shape
    tq, tk = FOX_TQ, FOX_TK
    return pl.pallas_call(
        _fox_kernel,
        grid=(b, HEADS // 2, s // tq),
        in_specs=[
            pl.BlockSpec((1, tq, 2 * LANES), lambda bi, j, qi: (bi, qi, j)),
            pl.BlockSpec((1, s, 4 * LANES), lambda bi, j, qi: (bi, 0, j)),
            pl.BlockSpec((1, 1, s // tk, LANES, tk), lambda bi, j, qi: (bi, j, 0, 0, 0)),
        ],
        out_specs=pl.BlockSpec((1, tq, LANES), lambda bi, j, qi: (bi, qi, j)),
        out_shape=jax.ShapeDtypeStruct((b, s, AW), F32),
        scratch_shapes=[pltpu.VMEM((SUBLANES, tq), F32), pltpu.VMEM((SUBLANES, tq), F32),
                        pltpu.VMEM((LANES, tq), F32)],
        compiler_params=_cparams(("arbitrary", "arbitrary", "arbitrary")),
        name="fox_attention",
    )(qa, ka, vt)


def _sb_consts():
    m = np.triu(np.ones((LANES, LANES), np.float32), 1)
    return jnp.asarray(np.concatenate([m, m], axis=1), BF16)


def _sb_attention(q, k, vt):
    b, s, _ = q.shape
    tq, tk = SB_TQ, SB_TK
    return pl.pallas_call(
        _sb_kernel,
        grid=(b, HEADS // 2, s // tq),
        in_specs=[
            pl.BlockSpec((LANES, 2 * LANES), lambda bi, j, qi: (0, 0)),
            pl.BlockSpec((1, tq, LANES), lambda bi, j, qi: (bi, qi, j)),
            pl.BlockSpec((1, s, 2 * LANES), lambda bi, j, qi: (bi, 0, j)),
            pl.BlockSpec((1, 1, s // tk, LANES, tk), lambda bi, j, qi: (bi, j, 0, 0, 0)),
        ],
        out_specs=pl.BlockSpec((1, tq, LANES), lambda bi, j, qi: (bi, qi, j)),
        out_shape=jax.ShapeDtypeStruct((b, s, AW), F32),
        scratch_shapes=[pltpu.VMEM((SUBLANES, tq), F32), pltpu.VMEM((LANES, tq), F32)],
        compiler_params=_cparams(("arbitrary", "arbitrary", "arbitrary")),
        name="sb_attention",
    )(_sb_consts(), q, k, vt)


def _decode_kernel(pt_ref, *refs, layer, n_new):
    npg = PAGES_PER_STEP
    fk_refs = refs[0:npg]
    fv_refs = refs[npg:2 * npg]
    lf_refs = refs[2 * npg:3 * npg]
    sk_refs = refs[3 * npg:4 * npg]
    sv_refs = refs[4 * npg:5 * npg]
    (qhat_ref, khat_ref, lfn_ref, vnew_ref, sq_ref, sknew_ref, svnew_ref, mst_ref,
     oa_ref, ob_ref,
     qf_ref, qsb_ref, m_ref, l_ref, cf_ref, cs_ref, accf_ref, accs_ref) = refs[5 * npg:]
    c = pl.program_id(1)
    rows = HEADS * n_new
    mst = mst_ref[...]
    mst3 = jnp.concatenate([mst, mst, mst], axis=0)
    mst2 = jnp.concatenate([mst, mst], axis=0)

    def q_rows(x):
        parts = [jnp.broadcast_to(x[t:t + 1, :], (HEADS, AW)) for t in range(n_new)]
        xr = jnp.concatenate(parts, axis=0)
        r = lax.broadcasted_iota(jnp.int32, xr.shape, 0)
        ln = lax.broadcasted_iota(jnp.int32, xr.shape, 1)
        return jnp.where((ln // HD) == (r % HEADS), xr, 0.0)

    def pad_rows(x):
        return jnp.concatenate([x, jnp.zeros((PAGE - n_new, x.shape[1]), x.dtype)], axis=0)

    def fox_update(st, bias_valid, v_bf, v_transposed):
        st = jnp.where(bias_valid, st, NEG_INF) if bias_valid is not None else st
        m_prev = m_ref[...]
        m_new = jnp.maximum(m_prev, jnp.max(st, axis=1, keepdims=True))
        alpha = jnp.exp(m_prev - m_new)
        p = jnp.exp(st - m_new)
        l_ref[...] = alpha * l_ref[...] + jnp.sum(p, axis=1, keepdims=True)
        m_ref[...] = m_new
        pv = _dot_nt(p.astype(BF16), v_bf) if v_transposed else _dot(p.astype(BF16), v_bf)
        accf_ref[...] = alpha * accf_ref[...] + pv

    def sb_update(z, valid, v_bf, v_transposed):
        lp = jnp.log(1.0 + jnp.exp(-jnp.abs(z)))
        lsig = jnp.minimum(z, 0.0) - lp
        lskip = lsig - z
        if valid is not None:
            lskip = jnp.where(valid, lskip, 0.0)
        hi, lo = _split2(lskip)
        after = _dot(jnp.concatenate([hi, lo], axis=1), mst2) + cs_ref[...]
        a = jnp.exp(lsig + after)
        if valid is not None:
            a = jnp.where(valid, a, 0.0)
        av = _dot_nt(a.astype(BF16), v_bf) if v_transposed else _dot(a.astype(BF16), v_bf)
        accs_ref[...] = accs_ref[...] + av
        cs_ref[...] = cs_ref[...] + jnp.sum(lskip, axis=1, keepdims=True)

    @pl.when(c == 0)
    def _():
        qf = q_rows(qhat_ref[0]).astype(BF16)
        qsb = q_rows(sq_ref[0] * (HD ** -0.5)).astype(BF16)
        qf_ref[...] = qf
        qsb_ref[...] = qsb
        m_ref[...] = jnp.full_like(m_ref, NEG_INF)
        l_ref[...] = jnp.zeros_like(l_ref)
        cs_ref[...] = jnp.zeros_like(cs_ref)
        accf_ref[...] = jnp.zeros_like(accf_ref)
        accs_ref[...] = jnp.zeros_like(accs_ref)
        r = lax.broadcasted_iota(jnp.int32, (rows, PAGE), 0)
        j = lax.broadcasted_iota(jnp.int32, (rows, PAGE), 1)
        t = r // HEADS
        lfn = lfn_ref[0]
        hl = lax.broadcasted_iota(jnp.int32, (HEADS, LANES), 0)
        ll = lax.broadcasted_iota(jnp.int32, (HEADS, LANES), 1)
        cols = []
        for i in range(n_new):
            sel = jnp.sum(jnp.where(hl == ll, jnp.broadcast_to(lfn[i:i + 1, :], (HEADS, LANES)), 0.0),
                          axis=1, keepdims=True)
            cols.append(jnp.concatenate([sel] * n_new, axis=0))
        bias = jnp.zeros((rows, PAGE), F32)
        tot = jnp.zeros((rows, 1), F32)
        for i in range(n_new):
            bias = bias + jnp.where((j < i) & (i <= t), cols[i], 0.0)
            tot = tot + jnp.where(i <= t[:, 0:1], cols[i], 0.0)
        cf_ref[...] = tot
        st = _dot_nt(qf, pad_rows(khat_ref[0]).astype(BF16)) + bias
        fox_update(st, j <= t, pad_rows(vnew_ref[0]).astype(BF16), False)
        z = _dot_nt(qsb, pad_rows(sknew_ref[0]).astype(BF16))
        sb_update(z, j < t, pad_rows(svnew_ref[0]).astype(BF16), False)

    qf = qf_ref[...]
    sts = []
    carry = cf_ref[...]
    for i in range(npg):
        lf8 = lf_refs[i][0, 0]
        lf = jnp.concatenate([lf8] * n_new, axis=0)
        suffix = _dot(jnp.concatenate(_split3(lf), axis=1), mst3)
        st = _dot(qf, fk_refs[i][0, 0].astype(BF16))
        sts.append(st + (suffix + carry))
        carry = carry + jnp.sum(lf, axis=1, keepdims=True)
    cf_ref[...] = carry
    st_all = jnp.concatenate(sts, axis=1)
    m_prev = m_ref[...]
    m_new = jnp.maximum(m_prev, jnp.max(st_all, axis=1, keepdims=True))
    alpha = jnp.exp(m_prev - m_new)
    p = jnp.exp(st_all - m_new)
    l_ref[...] = alpha * l_ref[...] + jnp.sum(p, axis=1, keepdims=True)
    m_ref[...] = m_new
    p = p.astype(BF16)
    pv = _dot_nt(p[:, 0:PAGE], fv_refs[0][0, 0].astype(BF16))
    for i in range(1, npg):
        pv = pv + _dot_nt(p[:, PAGE * i: PAGE * (i + 1)], fv_refs[i][0, 0].astype(BF16))
    accf_ref[...] = alpha * accf_ref[...] + pv

    @pl.when(jnp.max(cs_ref[...]) > SB_DEAD_LN)
    def _():
        qsb = qsb_ref[...]
        z = jnp.concatenate([_dot(qsb, sk_refs[i][0, 0].astype(BF16)) for i in range(npg)], axis=1)
        lp = jnp.log(1.0 + jnp.exp(-jnp.abs(z)))
        lsig = jnp.minimum(z, 0.0) - lp
        lskip = lsig - z
        hi, lo = _split2(lskip)
        carry_s = cs_ref[...]
        afters = []
        for i in range(npg):
            ps = slice(PAGE * i, PAGE * (i + 1))
            afters.append(_dot(jnp.concatenate([hi[:, ps], lo[:, ps]], axis=1), mst2) + carry_s)
            carry_s = carry_s + jnp.sum(lskip[:, ps], axis=1, keepdims=True)
        cs_ref[...] = carry_s
        a = jnp.exp(lsig + jnp.concatenate(afters, axis=1)).astype(BF16)
        av = _dot_nt(a[:, 0:PAGE], sv_refs[0][0, 0].astype(BF16))
        for i in range(1, npg):
            av = av + _dot_nt(a[:, PAGE * i: PAGE * (i + 1)], sv_refs[i][0, 0].astype(BF16))
        accs_ref[...] = accs_ref[...] + av

    @pl.when(c == pl.num_programs(1) - 1)
    def _():
        r = lax.broadcasted_iota(jnp.int32, (rows, AW), 0)
        ln = lax.broadcasted_iota(jnp.int32, (rows, AW), 1)
        sel = (ln // HD) == (r % HEADS)
        of = jnp.where(sel, accf_ref[...] / l_ref[...], 0.0)
        os_ = jnp.where(sel, accs_ref[...], 0.0)
        for t in range(n_new):
            oa_ref[0, t:t + 1, :] = jnp.sum(of[HEADS * t: HEADS * (t + 1), :], axis=0, keepdims=True)
            ob_ref[0, t:t + 1, :] = jnp.sum(os_[HEADS * t: HEADS * (t + 1), :], axis=0, keepdims=True)


def _decode_attention(layer, page_table, pool_fk, pool_fv, pool_lft, pool_sk, pool_sv,
                      qhat, khat, lfn, u3):
    b, n_pages = page_table.shape
    n_new = qhat.shape[1]
    npg = PAGES_PER_STEP
    nsteps = n_pages // npg
    rows = HEADS * n_new

    def page_spec(shape, i):
        def imap(bi, c, pt, i=i):
            return (layer, pt[bi, n_pages - 1 - (c * npg + i)], 0, 0)
        return pl.BlockSpec(shape, imap)

    def per_batch(w, col=0):
        return pl.BlockSpec((1, n_new, w), lambda bi, c, pt, col=col, w=w: (bi, 0, col // w))

    in_specs = ([page_spec((1, 1, AW, PAGE), i) for i in range(npg)]
                + [page_spec((1, 1, AW, PAGE), i) for i in range(npg)]
                + [page_spec((1, 1, HEADS, PAGE), i) for i in range(npg)]
                + [page_spec((1, 1, AW, PAGE), i) for i in range(npg)]
                + [page_spec((1, 1, AW, PAGE), i) for i in range(npg)]
                + [per_batch(AW), per_batch(AW), per_batch(LANES),
                   per_batch(AW, C_FV), per_batch(AW, C_SQ), per_batch(AW, C_SK), per_batch(AW, C_SV),
                   pl.BlockSpec((PAGE, PAGE), lambda bi, c, pt: (0, 0))])
    grid_spec = pltpu.PrefetchScalarGridSpec(
        num_scalar_prefetch=1,
        grid=(b, nsteps),
        in_specs=in_specs,
        out_specs=[pl.BlockSpec((1, n_new, AW), lambda bi, c, pt: (bi, 0, 0)),
                   pl.BlockSpec((1, n_new, AW), lambda bi, c, pt: (bi, 0, 0))],
        scratch_shapes=[pltpu.VMEM((rows, AW), BF16), pltpu.VMEM((rows, AW), BF16),
                        pltpu.VMEM((rows, 1), F32), pltpu.VMEM((rows, 1), F32),
                        pltpu.VMEM((rows, 1), F32), pltpu.VMEM((rows, 1), F32),
                        pltpu.VMEM((rows, AW), F32), pltpu.VMEM((rows, AW), F32)],
    )
    mst = jnp.asarray(np.tril(np.ones((PAGE, PAGE), np.float32), -1), BF16)
    args = ([pool_fk] * npg + [pool_fv] * npg + [pool_lft] * npg + [pool_sk] * npg + [pool_sv] * npg
            + [qhat, khat, lfn, u3, u3, u3, u3, mst])
    return pl.pallas_call(
        functools.partial(_decode_kernel, layer=layer, n_new=n_new),
        grid_spec=grid_spec,
        out_shape=[jax.ShapeDtypeStruct((b, n_new, AW), F32), jax.ShapeDtypeStruct((b, n_new, AW), F32)],
        compiler_params=_cparams(("arbitrary", "arbitrary")),
        name="decode_attention",
    )(page_table, *args)


def _ssd_kernel(*refs, t_valid, has_init):
    if has_init:
        (xbc_ref, z_ref, sm_ref, h0_ref, cprev_ref, cw_ref, cb_ref, dtb_ref, alog_ref, dsk_ref, nw_ref,
         ex_ref, tri_ref, y_ref, hl_ref, ext_ref, ht_ref) = refs
    else:
        (xbc_ref, z_ref, sm_ref, cw_ref, cb_ref, dtb_ref, alog_ref, dsk_ref, nw_ref,
         ex_ref, tri_ref, y_ref, hl_ref, ext_ref, ht_ref) = refs
    T = SSM_CHUNK
    GW = SSM_W // SSM_GROUPS
    c = pl.program_id(1)

    @pl.when(c == 0)
    def _():
        if has_init:
            for g in range(SSM_GROUPS):
                ht_ref[g] = h0_ref[0, GW * g: GW * (g + 1), :].T
            ext_ref[0:SUBLANES, :] = cprev_ref[0]
        else:
            ht_ref[...] = jnp.zeros_like(ht_ref)
            ext_ref[0:SUBLANES, :] = jnp.zeros((SUBLANES, SSM_CONV_DIM), F32)

    if t_valid == T:
        ext_ref[SUBLANES:SUBLANES + T, :] = xbc_ref[...]
    else:
        ext_ref[SUBLANES:SUBLANES + T, :] = jnp.zeros((T, SSM_CONV_DIM), F32)
        ext_ref[SUBLANES:SUBLANES + t_valid, :] = xbc_ref[0]

    conv = cb_ref[...]
    for j in range(SSM_CONV):
        off = SUBLANES - (SSM_CONV - 1) + j
        conv = conv + ext_ref[off: off + T, :] * cw_ref[j:j + 1, :]
    act = _silu(conv)
    xs = act[:, :SSM_W]
    bm = act[:, SSM_W: SSM_W + SSM_GROUPS * SSM_STATE]
    cm = act[:, SSM_W + SSM_GROUPS * SSM_STATE:]

    sm = sm_ref[...] if t_valid == T else jnp.concatenate(
        [sm_ref[0], jnp.zeros((T - t_valid, LANES), F32)], axis=0)
    lane = lax.broadcasted_iota(jnp.int32, (T, LANES), 1)
    row = lax.broadcasted_iota(jnp.int32, (T, LANES), 0)
    live = (lane >= DT_LANE0) & (lane < DT_LANE0 + SSM_HEADS) & (row < t_valid)
    dt = jnp.where(live, _softplus(sm + dtb_ref[...]), 0.0)
    a = -jnp.exp(alog_ref[...])
    da = dt * a
    tri = tri_ref[...]
    p3 = _split3(da)
    acum = _dot(tri, p3[0]) + _dot(tri, p3[1]) + _dot(tri, p3[2])

    ex = ex_ref[...]
    acum_e = _dot3(_split3(acum), ex)
    dt_e = _dot3(_split3(dt), ex)
    total_e = acum_e[T - 1:T, :]
    exp_acum_e = jnp.exp(acum_e)
    wend_e = jnp.exp(total_e - acum_e) * dt_e
    cdecay_e = jnp.exp(total_e)

    xw = (xs * wend_e).astype(BF16)
    acum_t = acum.T
    dt_t = dt.T
    rr = lax.broadcasted_iota(jnp.int32, (T, T), 0)
    cc = lax.broadcasted_iota(jnp.int32, (T, T), 1)
    causal = rr >= cc
    xs_b = xs.astype(BF16)
    zero_b = jnp.zeros((T, LANES), BF16)

    y_parts = []
    for g in range(SSM_GROUPS):
        bg = bm[:, SSM_STATE * g: SSM_STATE * (g + 1)]
        cg = cm[:, SSM_STATE * g: SSM_STATE * (g + 1)].astype(BF16)
        cb = _dot_nt(cg, bg.astype(BF16))
        h_prev = ht_ref[g]
        yoff = _dot(cg, h_prev.astype(BF16)) * exp_acum_e[:, GW * g: GW * (g + 1)]
        st = _dot(bg.T.astype(BF16), xw[:, GW * g: GW * (g + 1)])
        ht_ref[g] = h_prev * cdecay_e[:, GW * g: GW * (g + 1)] + st
        hpg = SSM_HEADS // SSM_GROUPS
        for jp in range(hpg // 2):
            pair = (hpg // 2) * g + jp
            xp = xs_b[:, LANES * pair: LANES * (pair + 1)]
            yp = jnp.zeros((T, LANES), F32)
            for e in range(2):
                h = 2 * pair + e
                ln = DT_LANE0 + h
                seg = acum[:, ln:ln + 1] - acum_t[ln:ln + 1, :]
                dec = jnp.exp(jnp.where(causal, seg, NEG_INF))
                w = cb * dec * dt_t[ln:ln + 1, :]
                xh = jnp.where(_half_mask((T, LANES), e), xp, zero_b)
                yp = yp + _dot(w.astype(BF16), xh)
            y_parts.append(yp + yoff[:, LANES * jp: LANES * (jp + 1)])
    y = jnp.concatenate(y_parts, axis=1)
    y = y + xs * dsk_ref[...]
    zz = z_ref[...] if t_valid == T else jnp.concatenate(
        [z_ref[0], jnp.zeros((T - t_valid, SSM_W), F32)], axis=0)
    y = y * _silu(zz)
    outs = []
    for g in range(SSM_GROUPS):
        yg = y[:, GW * g: GW * (g + 1)]
        ms = jnp.mean(yg * yg, axis=-1, keepdims=True)
        outs.append(yg * lax.rsqrt(ms + NORM_EPS) * nw_ref[:, GW * g: GW * (g + 1)])
    yo = jnp.concatenate(outs, axis=1)
    if t_valid == T:
        y_ref[...] = yo
    else:
        y_ref[0] = yo[:t_valid, :]

    ext_ref[0:SUBLANES, :] = ext_ref[T: T + SUBLANES, :]

    @pl.when(c == pl.num_programs(1) - 1)
    def _():
        for g in range(SSM_GROUPS):
            hl_ref[0, GW * g: GW * (g + 1), :] = ht_ref[g].T


def _ssd_consts():
    ex = np.zeros((LANES, SSM_W), np.float32)
    for h in range(SSM_HEADS):
        ex[DT_LANE0 + h, SSM_HD * h: SSM_HD * (h + 1)] = 1.0
    tri = np.tril(np.ones((SSM_CHUNK, SSM_CHUNK), np.float32))
    return jnp.asarray(ex, BF16), jnp.asarray(tri, BF16)


def _ssd(u, b, s, params, h0=None, cprev=None):
    cw, cb, dtb, alog, dsk, nw = params
    ex, tri = _ssd_consts()
    T = SSM_CHUNK
    has_init = h0 is not None
    prompt = (s % T == 0)
    nc = s // T if prompt else 1
    t_valid = T if prompt else s

    def const(shape):
        return pl.BlockSpec(shape, lambda bi, c: (0,) * len(shape))

    if prompt:
        def ucol(col, w):
            return pl.BlockSpec((T, w), lambda bi, c, col=col, w=w: (bi * nc + c, col // w))
        y_spec = pl.BlockSpec((T, SSM_W), lambda bi, c: (bi * nc + c, 0))
        y_shape = jax.ShapeDtypeStruct((b * s, SSM_W), F32)
    else:
        def ucol(col, w):
            return pl.BlockSpec((1, s, w), lambda bi, c, col=col, w=w: (bi, 0, col // w))
        y_spec = pl.BlockSpec((1, s, SSM_W), lambda bi, c: (bi, 0, 0))
        y_shape = jax.ShapeDtypeStruct((b, s, SSM_W), F32)
    in_specs = [ucol(C_XBC, SSM_CONV_DIM), ucol(C_Z, SSM_W), ucol(C_SMALL, LANES)]
    args = [u, u, u]
    if has_init:
        in_specs += [pl.BlockSpec((1, SSM_W, SSM_STATE), lambda bi, c: (bi, 0, 0)),
                     pl.BlockSpec((1, SUBLANES, SSM_CONV_DIM), lambda bi, c: (bi, 0, 0))]
        args += [h0, cprev]
    in_specs += [const((SUBLANES, SSM_CONV_DIM)), const((1, SSM_CONV_DIM)), const((1, LANES)),
                 const((1, LANES)), const((1, SSM_W)), const((1, SSM_W)),
                 const((LANES, SSM_W)), const((T, T))]
    args += [cw, cb, dtb, alog, dsk, nw, ex, tri]
    return pl.pallas_call(
        functools.partial(_ssd_kernel, t_valid=t_valid, has_init=has_init),
        grid=(b, nc),
        in_specs=in_specs,
        out_specs=[y_spec, pl.BlockSpec((1, SSM_W, SSM_STATE), lambda bi, c: (bi, 0, 0))],
        out_shape=[y_shape, jax.ShapeDtypeStruct((b, SSM_W, SSM_STATE), F32)],
        scratch_shapes=[pltpu.VMEM((T + 2 * SUBLANES, SSM_CONV_DIM), F32),
                        pltpu.VMEM((SSM_GROUPS, SSM_STATE, SSM_W // SSM_GROUPS), F32)],
        compiler_params=_cparams(("arbitrary", "arbitrary")),
        name="ssd",
    )(*args)


def _merge_kernel(x_ref, oa_ref, fg_ref, ob_ref, sg_ref, yc_ref, ga_ref, gb_ref, gc_ref, bg_ref,
                  wpa_ref, wpb_ref, wpc_ref, wout_ref, out_ref):
    ya = (oa_ref[...] * _silu(fg_ref[...])).astype(BF16)
    yb = (ob_ref[...] * _silu(sg_ref[...])).astype(BF16)
    yc = yc_ref[...].astype(BF16)
    h = _sigmoid(ga_ref[...] + bg_ref[:, 0:D_MODEL]) * _dot(ya, wpa_ref[...])
    h = h + _sigmoid(gb_ref[...] + bg_ref[:, D_MODEL:2 * D_MODEL]) * _dot(yb, wpb_ref[...])
    h = h + _sigmoid(gc_ref[...] + bg_ref[:, 2 * D_MODEL:3 * D_MODEL]) * _dot(yc, wpc_ref[...])
    out_ref[...] = x_ref[...] + _dot(h.astype(BF16), wout_ref[...])


def _merge(x2d, u, oa, ob, yc, bg, wpa, wpb, wpc, wout, tm):
    n = x2d.shape[0]

    def rows(w, col=0):
        return pl.BlockSpec((tm, w), lambda i, col=col, w=w: (i, col // w))

    def const(shape):
        return pl.BlockSpec(shape, lambda i: (0,) * len(shape))

    return pl.pallas_call(
        _merge_kernel,
        grid=(n // tm,),
        in_specs=[rows(D_MODEL), rows(AW), rows(AW, C_FG), rows(AW), rows(AW, C_SG), rows(SSM_W),
                  rows(D_MODEL, C_GL), rows(D_MODEL, C_GL + D_MODEL), rows(D_MODEL, C_GL + 2 * D_MODEL),
                  const((1, 3 * D_MODEL)), const((AW, D_MODEL)), const((AW, D_MODEL)),
                  const((SSM_W, D_MODEL)), const((D_MODEL, D_MODEL))],
        out_specs=rows(D_MODEL),
        out_shape=jax.ShapeDtypeStruct((n, D_MODEL), F32),
        compiler_params=_cparams(("arbitrary",)),
        name="merge",
    )(x2d, oa, u, ob, u, yc, u, u, u, bg, wpa, wpb, wpc, wout)


def _regroup_w_in(w):
    d = w.shape[0]
    z = lambda n: jnp.zeros((d, n), w.dtype)
    fox = jnp.concatenate([w[:, 0:1536], w[:, 1544:2056]], axis=1)
    sb = w[:, 2056:4104]
    cz = w[:, 4104:5128]
    cxbc = w[:, 5128:6664]
    cdt = w[:, 6664:6680]
    gl = w[:, 6680:9752]
    ff = w[:, 1536:1544]
    small = jnp.concatenate([ff, cdt, z(C_XBC - C_SMALL - 24)], axis=1)
    return jnp.concatenate([fox, sb, small, cxbc, cz, gl], axis=1).astype(BF16)


def _pad_lanes(v, lane0, width=LANES):
    out = jnp.zeros((1, width), F32)
    return out.at[0, lane0: lane0 + v.shape[0]].set(v.astype(F32))


def kernel(x_prompt, x_sample, cache_fox_k, cache_fox_v, cache_fox_logf, cache_sb_k, cache_sb_v,
           state_ssm, state_conv, page_table, norm_g, w_in, b_forget, fox_q_norm, fox_k_norm,
           conv_w, conv_b, dt_bias, a_log, d_skip, ssm_norm, b_gate, w_pa, w_pb, w_pc, w_out):
    depth = norm_g.shape[0]
    bp, sp, _ = x_prompt.shape
    bs, ss, _ = x_sample.shape
    n_pool = cache_fox_k.shape[1]
    npr, nsm = bp * sp, bs * ss

    def pool_view(c):
        return jnp.transpose(c, (0, 1, 3, 4, 2)).reshape(depth, n_pool, AW, PAGE)

    def cache_leaf(t):
        return jnp.transpose(t.reshape(t.shape[0], HEADS, HD, t.shape[2]), (0, 3, 1, 2))

    pool_fk, pool_fv, pool_sk, pool_sv = (pool_view(c) for c in
                                          (cache_fox_k, cache_fox_v, cache_sb_k, cache_sb_v))
    pool_lft = jnp.swapaxes(cache_fox_logf, 2, 3)
    h0_all = state_ssm.reshape(depth, bs, SSM_W, SSM_STATE)
    cprev_all = jnp.pad(state_conv, ((0, 0), (0, 0), (SUBLANES - (SSM_CONV - 1), 0), (0, 0)))

    xp = x_prompt.reshape(npr, D_MODEL)
    xs = x_sample.reshape(nsm, D_MODEL)
    new_p, new_s = [], []
    for l in range(depth):
        w_pad = _regroup_w_in(w_in[l])
        g = norm_g[l].reshape(1, D_MODEL)
        gq = jnp.tile(fox_q_norm[l], HEADS).reshape(1, AW) * (HD ** -0.5)
        gq2 = gq * LOG2E
        gk = jnp.tile(fox_k_norm[l], HEADS).reshape(1, AW)
        bf = _pad_lanes(b_forget[l], 0)
        ssd_params = (jnp.pad(conv_w[l], ((0, SUBLANES - SSM_CONV), (0, 0))), conv_b[l].reshape(1, -1),
                      _pad_lanes(dt_bias[l], DT_LANE0), _pad_lanes(a_log[l], DT_LANE0),
                      jnp.repeat(d_skip[l], SSM_HD).reshape(1, SSM_W), ssm_norm[l].reshape(1, SSM_W))
        bg = b_gate[l].reshape(1, -1)
        wpa, wpb, wpc, wout = (w_pa[l].astype(BF16), w_pb[l].astype(BF16), w_pc[l].astype(BF16),
                               w_out[l].astype(BF16))

        u = _in_proj(xp, g, w_pad, tm=min(1024, npr))
        kt, vt, skt, svt, lft, qa, ka, vtf, qs, ks, vts = _prep_prompt(u, bp, sp, gq2, gk, bf)
        oa = _fox_attention(qa, ka, vtf).reshape(npr, AW)
        ob = _sb_attention(qs, ks, vts).reshape(npr, AW)
        yc, h_last = _ssd(u, bp, sp, ssd_params)
        u3 = u.reshape(bp, sp, U_W)
        new_p.append((cache_leaf(kt), cache_leaf(vt), jnp.transpose(lft, (0, 2, 1)),
                      cache_leaf(skt), cache_leaf(svt),
                      h_last.reshape(bp, SSM_HEADS, SSM_HD, SSM_STATE),
                      u3[:, sp - (SSM_CONV - 1):, C_XBC:C_XBC + SSM_CONV_DIM]))
        xp = _merge(xp, u, oa, ob, yc, bg, wpa, wpb, wpc, wout, tm=min(512, npr))

        us = _in_proj(xs, g, w_pad, tm=nsm)
        qhat_s, khat_s, logf_s = _prep_sample(us, gq, gk, bf)
        us3 = us.reshape(bs, ss, U_W)
        oa_s, ob_s = _decode_attention(l, page_table, pool_fk, pool_fv, pool_lft, pool_sk, pool_sv,
                                       qhat_s.reshape(bs, ss, AW), khat_s.reshape(bs, ss, AW),
                                       logf_s.reshape(bs, ss, LANES), us3)
        yc_s, h_last_s = _ssd(us3, bs, ss, ssd_params, h0=h0_all[l], cprev=cprev_all[l])
        conv_cat = jnp.concatenate([state_conv[l], us3[:, :, C_XBC:C_XBC + SSM_CONV_DIM]], axis=1)
        new_s.append((khat_s.reshape(bs, ss, HEADS, HD),
                      us3[:, :, C_FV:C_FV + AW].reshape(bs, ss, HEADS, HD),
                      logf_s[:, :HEADS].reshape(bs, ss, HEADS),
                      us3[:, :, C_SK:C_SK + AW].reshape(bs, ss, HEADS, HD),
                      us3[:, :, C_SV:C_SV + AW].reshape(bs, ss, HEADS, HD),
                      h_last_s.reshape(bs, SSM_HEADS, SSM_HD, SSM_STATE),
                      conv_cat[:, -(SSM_CONV - 1):]))
        xs = _merge(xs, us, oa_s.reshape(nsm, AW), ob_s.reshape(nsm, AW), yc_s.reshape(nsm, SSM_W),
                    bg, wpa, wpb, wpc, wout, tm=nsm)

    outs_p = [jnp.stack([st[i] for st in new_p]) for i in range(7)]
    outs_s = [jnp.stack([st[i] for st in new_s]) for i in range(7)]
    return (xp.reshape(bp, sp, D_MODEL), xs.reshape(bs, ss, D_MODEL), *outs_p, *outs_s)
```

```python
import functools

import numpy as np
import jax
import jax.numpy as jnp
from jax import lax
from jax.experimental import pallas as pl
from jax.experimental.pallas import tpu as pltpu

F32 = jnp.float32
BF16 = jnp.bfloat16

D_MODEL = 1024
HEADS = 8
HD = 64
AW = HEADS * HD
SSM_W = 1024
SSM_HEADS = 16
SSM_HD = 64
SSM_GROUPS = 2
SSM_STATE = 128
SSM_CONV = 4
SSM_CONV_DIM = SSM_W + 2 * SSM_GROUPS * SSM_STATE
SSM_CHUNK = 128
PAGE = 128
NORM_EPS = 1e-6
NEG_INF = float("-inf")

LANES = 128
SUBLANES = 8
VMEM_LIMIT = 56 * 1024 * 1024

C_FQ, C_FK, C_FV, C_FG = 0, 512, 1024, 1536
C_SQ, C_SK, C_SV, C_SG = 2048, 2560, 3072, 3584
C_SMALL = 4096
C_XBC = 4608
C_Z = 6144
C_GL = 7168
U_W = 10240
DT_LANE0 = 8
PAGES_PER_STEP = 8

FOX_TQ = 1024
FOX_TK = 512
SB_TQ = 256
SB_TK = 256
LOG2E = 1.4426950408889634
SB_DEAD_LOG2 = -151.0
SB_DEAD_LN = -104.0


def _cparams(sem):
    return pltpu.CompilerParams(dimension_semantics=sem, vmem_limit_bytes=VMEM_LIMIT)


def _logsig(x):
    return jnp.minimum(x, 0.0) - jnp.log1p(jnp.exp(-jnp.abs(x)))


def _softplus(x):
    return jnp.maximum(x, 0.0) + jnp.log1p(jnp.exp(-jnp.abs(x)))


def _sigmoid(x):
    return 1.0 / (1.0 + jnp.exp(-x))


def _silu(x):
    return x * _sigmoid(x)


def _split3(x):
    hi = x.astype(BF16)
    r1 = x - hi.astype(F32)
    mid = r1.astype(BF16)
    lo = (r1 - mid.astype(F32)).astype(BF16)
    return hi, mid, lo


def _split2(x):
    hi = x.astype(BF16)
    lo = (x - hi.astype(F32)).astype(BF16)
    return hi, lo


def _dot(a, b):
    return jnp.dot(a, b, preferred_element_type=F32)


def _dot_nt(a, b):
    return lax.dot_general(a, b, (((1,), (1,)), ((), ())), preferred_element_type=F32)


def _dot3(pieces, m):
    out = _dot(pieces[0], m)
    for p in pieces[1:]:
        out = out + _dot(p, m)
    return out


def _in_proj_kernel(x_ref, g_ref, w_ref, u_ref, xn_ref):
    @pl.when(pl.program_id(1) == 0)
    def _():
        x = x_ref[...]
        ms = jnp.mean(x * x, axis=-1, keepdims=True)
        xn_ref[...] = (x * lax.rsqrt(ms + NORM_EPS) * g_ref[...]).astype(BF16)

    u_ref[...] = _dot(xn_ref[...], w_ref[...])


def _in_proj(x2d, g, w_pad, tm, tn=2048):
    n = x2d.shape[0]
    return pl.pallas_call(
        _in_proj_kernel,
        grid=(n // tm, U_W // tn),
        in_specs=[
            pl.BlockSpec((tm, D_MODEL), lambda i, j: (i, 0)),
            pl.BlockSpec((1, D_MODEL), lambda i, j: (0, 0)),
            pl.BlockSpec((D_MODEL, tn), lambda i, j: (0, j)),
        ],
        out_specs=pl.BlockSpec((tm, tn), lambda i, j: (i, j)),
        out_shape=jax.ShapeDtypeStruct((n, U_W), F32),
        scratch_shapes=[pltpu.VMEM((tm, D_MODEL), BF16)],
        compiler_params=_cparams(("arbitrary", "arbitrary")),
        name="in_proj",
    )(x2d, g, w_pad)


def _head_norm(x, gain, bd):
    xx = x * x
    hi, lo = _split2(xx)
    ss = _dot(hi, bd) + _dot(lo, bd)
    return x * lax.rsqrt(ss * (1.0 / HD) + NORM_EPS) * gain


def _half_mask(shape, head):
    lane = lax.broadcasted_iota(jnp.int32, shape, len(shape) - 1)
    return (lane >= HD) if (head % 2) else (lane < HD)


def _prep_prompt_kernel(fq_ref, fk_ref, fv_ref, sq_ref, sk_ref, sv_ref, sm_ref,
                        gq_ref, gk_ref, bf_ref, bd_ref, tri_ref, sqm_ref, skm_ref, oq_ref, ok_ref,
                        kt_ref, vt_ref, skt_ref, svt_ref, lft_ref, qa_ref, ka_ref, vtf_ref, qs_ref, ks_ref,
                        vts_ref, carry_ref):
    @pl.when(pl.program_id(1) == 0)
    def _():
        carry_ref[...] = jnp.zeros_like(carry_ref)

    bd = bd_ref[...]
    q = _head_norm(fq_ref[...], gq_ref[...], bd)
    k = _head_norm(fk_ref[...], gk_ref[...], bd)

    sm = sm_ref[...]
    lane = lax.broadcasted_iota(jnp.int32, sm.shape, 1)
    lf = jnp.where(lane < HEADS, _logsig(sm + bf_ref[...]), 0.0)
    lft_ref[0] = lf.T[0:HEADS, :]

    tri = tri_ref[...]
    lf3 = _split3(lf)
    cum = _dot(tri, lf3[0]) + _dot(tri, lf3[1]) + _dot(tri, lf3[2]) + carry_ref[...]
    tm = cum.shape[0]
    carry_ref[...] = cum[tm - 1:tm, :]

    cat = jnp.concatenate(_split3(cum * LOG2E), axis=1)
    aug_q = (_dot(cat, sqm_ref[...]) + oq_ref[...]).astype(BF16)
    aug_k = (_dot(cat, skm_ref[...]) + ok_ref[...]).astype(BF16)

    qb = q.astype(BF16)
    kb = k.astype(BF16)
    fv = fv_ref[...]
    sq = (sq_ref[...] * (HD ** -0.5 * LOG2E)).astype(BF16)
    sk32 = sk_ref[...]
    sk = sk32.astype(BF16)
    sv = sv_ref[...]
    qs_ref[0] = sq
    zero = jnp.zeros((tm, LANES), BF16)
    for j in range(HEADS // 2):
        ls = slice(LANES * j, LANES * (j + 1))
        qa_ref[0, :, 2 * LANES * j: 2 * LANES * j + LANES] = qb[:, ls]
        qa_ref[0, :, 2 * LANES * j + LANES: 2 * LANES * (j + 1)] = aug_q[:, ls]
        fvt = fv[:, ls].T
        svt = sv[:, ls].T
        vt_ref[0, ls, :] = fvt
        svt_ref[0, ls, :] = svt
        for c in range(tm // FOX_TK):
            vtf_ref[0, j, c] = fvt[:, FOX_TK * c: FOX_TK * (c + 1)].astype(BF16)
        for c in range(tm // SB_TK):
            vts_ref[0, j, c] = svt[:, SB_TK * c: SB_TK * (c + 1)].astype(BF16)
        kt_ref[0, ls, :] = k[:, ls].T
        skt_ref[0, ls, :] = sk32[:, ls].T
        for e in range(2):
            h = 2 * j + e
            hm = _half_mask((tm, LANES), h)
            ka_ref[0, :, 2 * LANES * h: 2 * LANES * h + LANES] = jnp.where(hm, kb[:, ls], zero)
            ka_ref[0, :, 2 * LANES * h + LANES: 2 * LANES * (h + 1)] = aug_k[:, LANES * h: LANES * (h + 1)]
            ks_ref[0, :, LANES * h: LANES * (h + 1)] = jnp.where(hm, sk[:, ls], zero)


def _prep_consts(tm):
    bd = np.kron(np.eye(HEADS, dtype=np.float32), np.ones((HD, HD), np.float32))
    tri = np.tril(np.ones((tm, tm), np.float32))
    sqm = np.zeros((3 * LANES, AW), np.float32)
    skm = np.zeros((3 * LANES, 2 * AW), np.float32)
    oq = np.zeros((1, AW), np.float32)
    ok = np.zeros((1, 2 * AW), np.float32)
    for h in range(HEADS):
        base = (h % 2) * 6
        for i in range(3):
            sqm[i * LANES + h, (h // 2) * LANES + base + i] = 1.0
            skm[i * LANES + h, h * LANES + base + 3 + i] = -1.0
            oq[0, (h // 2) * LANES + base + 3 + i] = 1.0
            ok[0, h * LANES + base + i] = 1.0
    return (jnp.asarray(bd, BF16), jnp.asarray(tri, BF16), jnp.asarray(sqm, BF16),
            jnp.asarray(skm, BF16), jnp.asarray(oq), jnp.asarray(ok))


def _prep_prompt(u, b, s, gq, gk, bf, tm=512):
    nt = s // tm
    bd, tri, sqm, skm, oq, ok = _prep_consts(tm)

    def ucol(c, w):
        return pl.BlockSpec((tm, w), lambda bi, i, c=c, w=w: (bi * nt + i, c // w))

    def const(shape):
        return pl.BlockSpec(shape, lambda bi, i: (0,) * len(shape))

    outs = pl.pallas_call(
        _prep_prompt_kernel,
        grid=(b, nt),
        in_specs=[ucol(C_FQ, AW), ucol(C_FK, AW), ucol(C_FV, AW), ucol(C_SQ, AW), ucol(C_SK, AW),
                  ucol(C_SV, AW), ucol(C_SMALL, LANES),
                  const((1, AW)), const((1, AW)), const((1, LANES)), const((AW, AW)), const((tm, tm)),
                  const((3 * LANES, AW)), const((3 * LANES, 2 * AW)), const((1, AW)), const((1, 2 * AW))],
        out_specs=[
            pl.BlockSpec((1, AW, tm), lambda bi, i: (bi, 0, i)),
            pl.BlockSpec((1, AW, tm), lambda bi, i: (bi, 0, i)),
            pl.BlockSpec((1, AW, tm), lambda bi, i: (bi, 0, i)),
            pl.BlockSpec((1, AW, tm), lambda bi, i: (bi, 0, i)),
            pl.BlockSpec((1, HEADS, tm), lambda bi, i: (bi, 0, i)),
            pl.BlockSpec((1, tm, 2 * AW), lambda bi, i: (bi, i, 0)),
            pl.BlockSpec((1, tm, 4 * AW), lambda bi, i: (bi, i, 0)),
            pl.BlockSpec((1, HEADS // 2, tm // FOX_TK, LANES, FOX_TK), lambda bi, i: (bi, 0, i, 0, 0)),
            pl.BlockSpec((1, tm, AW), lambda bi, i: (bi, i, 0)),
            pl.BlockSpec((1, tm, 2 * AW), lambda bi, i: (bi, i, 0)),
            pl.BlockSpec((1, HEADS // 2, tm // SB_TK, LANES, SB_TK), lambda bi, i: (bi, 0, i, 0, 0)),
        ],
        out_shape=[
            jax.ShapeDtypeStruct((b, AW, s), F32),
            jax.ShapeDtypeStruct((b, AW, s), F32),
            jax.ShapeDtypeStruct((b, AW, s), F32),
            jax.ShapeDtypeStruct((b, AW, s), F32),
            jax.ShapeDtypeStruct((b, HEADS, s), F32),
            jax.ShapeDtypeStruct((b, s, 2 * AW), BF16),
            jax.ShapeDtypeStruct((b, s, 4 * AW), BF16),
            jax.ShapeDtypeStruct((b, HEADS // 2, s // FOX_TK, LANES, FOX_TK), BF16),
            jax.ShapeDtypeStruct((b, s, AW), BF16),
            jax.ShapeDtypeStruct((b, s, 2 * AW), BF16),
            jax.ShapeDtypeStruct((b, HEADS // 2, s // SB_TK, LANES, SB_TK), BF16),
        ],
        scratch_shapes=[pltpu.VMEM((1, LANES), F32)],
        compiler_params=_cparams(("arbitrary", "arbitrary")),
        name="prep_prompt",
    )(u, u, u, u, u, u, u, gq, gk, bf, bd, tri, sqm, skm, oq, ok)
    return outs


def _prep_sample_kernel(fq_ref, fk_ref, sm_ref, gq_ref, gk_ref, bf_ref, bd_ref,
                        qhat_ref, khat_ref, logf_ref):
    bd = bd_ref[...]
    qhat_ref[...] = _head_norm(fq_ref[...], gq_ref[...], bd)
    khat_ref[...] = _head_norm(fk_ref[...], gk_ref[...], bd)
    sm = sm_ref[...]
    lane = lax.broadcasted_iota(jnp.int32, sm.shape, 1)
    logf_ref[...] = jnp.where(lane < HEADS, _logsig(sm + bf_ref[...]), 0.0)


def _prep_sample(u, gq, gk, bf):
    n = u.shape[0]
    bd = _prep_consts(8)[0]

    def ucol(c, w):
        return pl.BlockSpec((n, w), lambda i, c=c, w=w: (0, c // w))

    def const(shape):
        return pl.BlockSpec(shape, lambda i: (0,) * len(shape))

    return pl.pallas_call(
        _prep_sample_kernel,
        grid=(1,),
        in_specs=[ucol(C_FQ, AW), ucol(C_FK, AW), ucol(C_SMALL, LANES),
                  const((1, AW)), const((1, AW)), const((1, LANES)), const((AW, AW))],
        out_specs=[pl.BlockSpec((n, AW), lambda i: (0, 0)), pl.BlockSpec((n, AW), lambda i: (0, 0)),
                   pl.BlockSpec((n, LANES), lambda i: (0, 0))],
        out_shape=[jax.ShapeDtypeStruct((n, AW), F32), jax.ShapeDtypeStruct((n, AW), F32),
                   jax.ShapeDtypeStruct((n, LANES), F32)],
        compiler_params=_cparams(("arbitrary",)),
        name="prep_sample",
    )(u, u, u, gq, gk, bf, bd)


def _fox_kernel(qa_ref, ka_ref, vt_ref, o_ref, m_ref, l_ref, acc_ref):
    qi = pl.program_id(2)
    tq, tk = FOX_TQ, FOX_TK
    m_ref[...] = jnp.full_like(m_ref, NEG_INF)
    l_ref[...] = jnp.zeros_like(l_ref)
    acc_ref[...] = jnp.zeros_like(acc_ref)
    q = qa_ref[0]

    def block(ki, masked):
        k0 = pl.multiple_of(ki * tk, tk)
        for e in range(2):
            k = ka_ref[0, pl.ds(k0, tk), 2 * LANES * e: 2 * LANES * (e + 1)]
            st = _dot_nt(k, q)
            if masked is not None:
                r = lax.broadcasted_iota(jnp.int32, st.shape, 0) + tk * masked
                c = lax.broadcasted_iota(jnp.int32, st.shape, 1)
                st = jnp.where(r <= c, st, NEG_INF)
            m_prev = m_ref[e:e + 1, :]
            m_new = jnp.maximum(m_prev, jnp.max(st, axis=0, keepdims=True))
            alpha = jnp.exp2(m_prev - m_new)
            p = jnp.exp2(st - m_new)
            l_ref[e:e + 1, :] = alpha * l_ref[e:e + 1, :] + jnp.sum(p, axis=0, keepdims=True)
            m_ref[e:e + 1, :] = m_new
            pv = _dot(vt_ref[0, 0, ki, HD * e: HD * (e + 1), :], p.astype(BF16))
            acc_ref[HD * e: HD * (e + 1), :] = alpha * acc_ref[HD * e: HD * (e + 1), :] + pv

    def loop_body(ki, carry):
        block(ki, None)
        return carry

    nd = tq // tk
    lax.fori_loop(0, qi * nd, loop_body, 0)
    for d in range(nd):
        block(qi * nd + d, d)
    o0 = acc_ref[0:HD, :] / l_ref[0:1, :]
    o1 = acc_ref[HD:2 * HD, :] / l_ref[1:2, :]
    o_ref[0] = jnp.concatenate([o0, o1], axis=0).T


def _sb_kernel(mt2_ref, q_ref, k_ref, vt_ref, o_ref, carry_ref, acc_ref):
    qi = pl.program_id(2)
    tq, tk = SB_TQ, SB_TK
    nsub = tk // LANES
    carry_ref[...] = jnp.zeros_like(carry_ref)
    acc_ref[...] = jnp.zeros_like(acc_ref)
    mt2 = mt2_ref[...]
    q = q_ref[0]

    def block(ki, masked):
        k0 = pl.multiple_of(ki * tk, tk)
        for e in range(2):
            k = k_ref[0, pl.ds(k0, tk), LANES * e: LANES * (e + 1)]
            zt = _dot_nt(k, q)
            carry = carry_ref[e:e + 1, :]
            a_parts = [None] * nsub
            for sub in range(nsub - 1, -1, -1):
                z = zt[LANES * sub: LANES * (sub + 1), :]
                lp = jnp.log2(1.0 + jnp.exp2(-jnp.abs(z)))
                lsig = jnp.minimum(z, 0.0) - lp
                lskip = lsig - z
                if masked:
                    r = lax.broadcasted_iota(jnp.int32, z.shape, 0) + LANES * sub
                    c = lax.broadcasted_iota(jnp.int32, z.shape, 1)
                    valid = r < c
                    lskip = jnp.where(valid, lskip, 0.0)
                hi, lo = _split2(lskip)
                after = _dot(mt2, jnp.concatenate([hi, lo], axis=0)) + carry
                a = jnp.exp2(lsig + after)
                if masked:
                    a = jnp.where(valid, a, 0.0)
                a_parts[sub] = a.astype(BF16)
                carry = carry + jnp.sum(lskip, axis=0, keepdims=True)
            carry_ref[e:e + 1, :] = carry
            av = _dot(vt_ref[0, 0, ki, HD * e: HD * (e + 1), :], jnp.concatenate(a_parts, axis=0))
            acc_ref[HD * e: HD * (e + 1), :] = acc_ref[HD * e: HD * (e + 1), :] + av

    block(qi, True)
    carry_ref[...] = jnp.where(qi > 0, carry_ref[...], -1e30)
    block(jnp.maximum(qi - 1, 0), False)

    def cond(ki):
        return jnp.logical_and(ki >= 0, jnp.max(carry_ref[0:2, :]) > SB_DEAD_LOG2)

    def loop_body(ki):
        block(ki, False)
        return ki - 1

    lax.while_loop(cond, loop_body, qi - 2)
    o_ref[0] = acc_ref[...].T


def _fox_attention(qa, ka, vt):
    b, s, _ = qa.shape
    tq, tk = FOX_TQ, FOX_TK
    return pl.pallas_call(
        _fox_kernel,
        grid=(b, HEADS // 2, s // tq),
        in_specs=[
            pl.BlockSpec((1, tq, 2 * LANES), lambda bi, j, qi: (bi, qi, j)),
            pl.BlockSpec((1, s, 4 * LANES), lambda bi, j, qi: (bi, 0, j)),
            pl.BlockSpec((1, 1, s // tk, LANES, tk), lambda bi, j, qi: (bi, j, 0, 0, 0)),
        ],
        out_specs=pl.BlockSpec((1, tq, LANES), lambda bi, j, qi: (bi, qi, j)),
        out_shape=jax.ShapeDtypeStruct((b, s, AW), F32),
        scratch_shapes=[pltpu.VMEM((SUBLANES, tq), F32), pltpu.VMEM((SUBLANES, tq), F32),
                        pltpu.VMEM((LANES, tq), F32)],
        compiler_params=_cparams(("arbitrary", "arbitrary", "arbitrary")),
        name="fox_attention",
    )(qa, ka, vt)


def _sb_consts():
    m = np.triu(np.ones((LANES, LANES), np.float32), 1)
    return jnp.asarray(np.concatenate([m, m], axis=1), BF16)


def _sb_attention(q, k, vt):
    b, s, _ = q.shape
    tq, tk = SB_TQ, SB_TK
    return pl.pallas_call(
        _sb_kernel,
        grid=(b, HEADS // 2, s // tq),
        in_specs=[
            pl.BlockSpec((LANES, 2 * LANES), lambda bi, j, qi: (0, 0)),
            pl.BlockSpec((1, tq, LANES), lambda bi, j, qi: (bi, qi, j)),
            pl.BlockSpec((1, s, 2 * LANES), lambda bi, j, qi: (bi, 0, j)),
            pl.BlockSpec((1, 1, s // tk, LANES, tk), lambda bi, j, qi: (bi, j, 0, 0, 0)),
        ],
        out_specs=pl.BlockSpec((1, tq, LANES), lambda bi, j, qi: (bi, qi, j)),
        out_shape=jax.ShapeDtypeStruct((b, s, AW), F32),
        scratch_shapes=[pltpu.VMEM((SUBLANES, tq), F32), pltpu.VMEM((LANES, tq), F32)],
        compiler_params=_cparams(("arbitrary", "arbitrary", "arbitrary")),
        name="sb_attention",
    )(_sb_consts(), q, k, vt)


def _decode_kernel(pt_ref, *refs, layer, n_new, n_pages):
    npg = PAGES_PER_STEP
    fk_refs = refs[0:npg]
    fv_refs = refs[npg:2 * npg]
    lf_refs = refs[2 * npg:3 * npg]
    (skpool_ref, svpool_ref,
     qhat_ref, khat_ref, lfn_ref, vnew_ref, sq_ref, sknew_ref, svnew_ref, mst_ref,
     oa_ref, ob_ref,
     qf_ref, qsb_ref, m_ref, l_ref, cf_ref, cs_ref, accf_ref, accs_ref,
     skbuf_ref, svbuf_ref, sem_ref) = refs[3 * npg:]
    bi = pl.program_id(0)
    c = pl.program_id(1)
    rows = HEADS * n_new

    def sb_page_copies():
        cps = []
        for i in range(npg):
            page = pt_ref[bi, n_pages - 1 - (c * npg + i)]
            cps.append(pltpu.make_async_copy(skpool_ref.at[layer, page], skbuf_ref.at[i], sem_ref.at[0]))
            cps.append(pltpu.make_async_copy(svpool_ref.at[layer, page], svbuf_ref.at[i], sem_ref.at[1]))
        return cps
    mst = mst_ref[...]
    mst3 = jnp.concatenate([mst, mst, mst], axis=0)
    mst2 = jnp.concatenate([mst, mst], axis=0)

    def q_rows(x):
        parts = [jnp.broadcast_to(x[t:t + 1, :], (HEADS, AW)) for t in range(n_new)]
        xr = jnp.concatenate(parts, axis=0)
        r = lax.broadcasted_iota(jnp.int32, xr.shape, 0)
        ln = lax.broadcasted_iota(jnp.int32, xr.shape, 1)
        return jnp.where((ln // HD) == (r % HEADS), xr, 0.0)

    def pad_rows(x):
        return jnp.concatenate([x, jnp.zeros((PAGE - n_new, x.shape[1]), x.dtype)], axis=0)

    def fox_update(st, bias_valid, v_bf, v_transposed):
        st = jnp.where(bias_valid, st, NEG_INF) if bias_valid is not None else st
        m_prev = m_ref[...]
        m_new = jnp.maximum(m_prev, jnp.max(st, axis=1, keepdims=True))
        alpha = jnp.exp(m_prev - m_new)
        p = jnp.exp(st - m_new)
        l_ref[...] = alpha * l_ref[...] + jnp.sum(p, axis=1, keepdims=True)
        m_ref[...] = m_new
        pv = _dot_nt(p.astype(BF16), v_bf) if v_transposed else _dot(p.astype(BF16), v_bf)
        accf_ref[...] = alpha * accf_ref[...] + pv

    def sb_update(z, valid, v_bf, v_transposed):
        lp = jnp.log(1.0 + jnp.exp(-jnp.abs(z)))
        lsig = jnp.minimum(z, 0.0) - lp
        lskip = lsig - z
        if valid is not None:
            lskip = jnp.where(valid, lskip, 0.0)
        hi, lo = _split2(lskip)
        after = _dot(jnp.concatenate([hi, lo], axis=1), mst2) + cs_ref[...]
        a = jnp.exp(lsig + after)
        if valid is not None:
            a = jnp.where(valid, a, 0.0)
        av = _dot_nt(a.astype(BF16), v_bf) if v_transposed else _dot(a.astype(BF16), v_bf)
        accs_ref[...] = accs_ref[...] + av
        cs_ref[...] = cs_ref[...] + jnp.sum(lskip, axis=1, keepdims=True)

    @pl.when(c == 0)
    def _():
        qf = q_rows(qhat_ref[0]).astype(BF16)
        qsb = q_rows(sq_ref[0] * (HD ** -0.5)).astype(BF16)
        qf_ref[...] = qf
        qsb_ref[...] = qsb
        m_ref[...] = jnp.full_like(m_ref, NEG_INF)
        l_ref[...] = jnp.zeros_like(l_ref)
        cs_ref[...] = jnp.zeros_like(cs_ref)
        accf_ref[...] = jnp.zeros_like(accf_ref)
        accs_ref[...] = jnp.zeros_like(accs_ref)
        r = lax.broadcasted_iota(jnp.int32, (rows, PAGE), 0)
        j = lax.broadcasted_iota(jnp.int32, (rows, PAGE), 1)
        t = r // HEADS
        lfn = lfn_ref[0]
        hl = lax.broadcasted_iota(jnp.int32, (HEADS, LANES), 0)
        ll = lax.broadcasted_iota(jnp.int32, (HEADS, LANES), 1)
        cols = []
        for i in range(n_new):
            sel = jnp.sum(jnp.where(hl == ll, jnp.broadcast_to(lfn[i:i + 1, :], (HEADS, LANES)), 0.0),
                          axis=1, keepdims=True)
            cols.append(jnp.concatenate([sel] * n_new, axis=0))
        bias = jnp.zeros((rows, PAGE), F32)
        tot = jnp.zeros((rows, 1), F32)
        for i in range(n_new):
            bias = bias + jnp.where((j < i) & (i <= t), cols[i], 0.0)
            tot = tot + jnp.where(i <= t[:, 0:1], cols[i], 0.0)
        cf_ref[...] = tot
        st = _dot_nt(qf, pad_rows(khat_ref[0]).astype(BF16)) + bias
        fox_update(st, j <= t, pad_rows(vnew_ref[0]).astype(BF16), False)
        z = _dot_nt(qsb, pad_rows(sknew_ref[0]).astype(BF16))
        sb_update(z, j < t, pad_rows(svnew_ref[0]).astype(BF16), False)

    sb_alive = jnp.max(cs_ref[...]) > SB_DEAD_LN

    @pl.when(sb_alive)
    def _():
        for cp in sb_page_copies():
            cp.start()

    qf = qf_ref[...]
    sts = []
    carry = cf_ref[...]
    for i in range(npg):
        lf8 = lf_refs[i][0, 0]
        lf = jnp.concatenate([lf8] * n_new, axis=0)
        suffix = _dot(jnp.concatenate(_split3(lf), axis=1), mst3)
        st = _dot(qf, fk_refs[i][0, 0].astype(BF16))
        sts.append(st + (suffix + carry))
        carry = carry + jnp.sum(lf, axis=1, keepdims=True)
    cf_ref[...] = carry
    st_all = jnp.concatenate(sts, axis=1)
    m_prev = m_ref[...]
    m_new = jnp.maximum(m_prev, jnp.max(st_all, axis=1, keepdims=True))
    alpha = jnp.exp(m_prev - m_new)
    p = jnp.exp(st_all - m_new)
    l_ref[...] = alpha * l_ref[...] + jnp.sum(p, axis=1, keepdims=True)
    m_ref[...] = m_new
    p = p.astype(BF16)
    pv = _dot_nt(p[:, 0:PAGE], fv_refs[0][0, 0].astype(BF16))
    for i in range(1, npg):
        pv = pv + _dot_nt(p[:, PAGE * i: PAGE * (i + 1)], fv_refs[i][0, 0].astype(BF16))
    accf_ref[...] = alpha * accf_ref[...] + pv

    @pl.when(sb_alive)
    def _():
        for cp in sb_page_copies():
            cp.wait()
        qsb = qsb_ref[...]
        z = jnp.concatenate([_dot(qsb, skbuf_ref[i].astype(BF16)) for i in range(npg)], axis=1)
        lp = jnp.log(1.0 + jnp.exp(-jnp.abs(z)))
        lsig = jnp.minimum(z, 0.0) - lp
        lskip = lsig - z
        hi, lo = _split2(lskip)
        carry_s = cs_ref[...]
        afters = []
        for i in range(npg):
            ps = slice(PAGE * i, PAGE * (i + 1))
            afters.append(_dot(jnp.concatenate([hi[:, ps], lo[:, ps]], axis=1), mst2) + carry_s)
            carry_s = carry_s + jnp.sum(lskip[:, ps], axis=1, keepdims=True)
        cs_ref[...] = carry_s
        a = jnp.exp(lsig + jnp.concatenate(afters, axis=1)).astype(BF16)
        av = _dot_nt(a[:, 0:PAGE], svbuf_ref[0].astype(BF16))
        for i in range(1, npg):
            av = av + _dot_nt(a[:, PAGE * i: PAGE * (i + 1)], svbuf_ref[i].astype(BF16))
        accs_ref[...] = accs_ref[...] + av

    @pl.when(c == pl.num_programs(1) - 1)
    def _():
        r = lax.broadcasted_iota(jnp.int32, (rows, AW), 0)
        ln = lax.broadcasted_iota(jnp.int32, (rows, AW), 1)
        sel = (ln // HD) == (r % HEADS)
        of = jnp.where(sel, accf_ref[...] / l_ref[...], 0.0)
        os_ = jnp.where(sel, accs_ref[...], 0.0)
        for t in range(n_new):
            oa_ref[0, t:t + 1, :] = jnp.sum(of[HEADS * t: HEADS * (t + 1), :], axis=0, keepdims=True)
            ob_ref[0, t:t + 1, :] = jnp.sum(os_[HEADS * t: HEADS * (t + 1), :], axis=0, keepdims=True)


def _decode_attention(layer, page_table, pool_fk, pool_fv, pool_lft, pool_sk, pool_sv,
                      qhat, khat, lfn, u3):
    b, n_pages = page_table.shape
    n_new = qhat.shape[1]
    npg = PAGES_PER_STEP
    nsteps = n_pages // npg
    rows = HEADS * n_new

    def page_spec(shape, i):
        def imap(bi, c, pt, i=i):
            return (layer, pt[bi, n_pages - 1 - (c * npg + i)], 0, 0)
        return pl.BlockSpec(shape, imap)

    def per_batch(w, col=0):
        return pl.BlockSpec((1, n_new, w), lambda bi, c, pt, col=col, w=w: (bi, 0, col // w))

    in_specs = ([page_spec((1, 1, AW, PAGE), i) for i in range(npg)]
                + [page_spec((1, 1, AW, PAGE), i) for i in range(npg)]
                + [page_spec((1, 1, HEADS, PAGE), i) for i in range(npg)]
                + [pl.BlockSpec(memory_space=pl.ANY), pl.BlockSpec(memory_space=pl.ANY)]
                + [per_batch(AW), per_batch(AW), per_batch(LANES),
                   per_batch(AW, C_FV), per_batch(AW, C_SQ), per_batch(AW, C_SK), per_batch(AW, C_SV),
                   pl.BlockSpec((PAGE, PAGE), lambda bi, c, pt: (0, 0))])
    grid_spec = pltpu.PrefetchScalarGridSpec(
        num_scalar_prefetch=1,
        grid=(b, nsteps),
        in_specs=in_specs,
        out_specs=[pl.BlockSpec((1, n_new, AW), lambda bi, c, pt: (bi, 0, 0)),
                   pl.BlockSpec((1, n_new, AW), lambda bi, c, pt: (bi, 0, 0))],
        scratch_shapes=[pltpu.VMEM((rows, AW), BF16), pltpu.VMEM((rows, AW), BF16),
                        pltpu.VMEM((rows, 1), F32), pltpu.VMEM((rows, 1), F32),
                        pltpu.VMEM((rows, 1), F32), pltpu.VMEM((rows, 1), F32),
                        pltpu.VMEM((rows, AW), F32), pltpu.VMEM((rows, AW), F32),
                        pltpu.VMEM((npg, AW, PAGE), F32), pltpu.VMEM((npg, AW, PAGE), F32),
                        pltpu.SemaphoreType.DMA((2,))],
    )
    mst = jnp.asarray(np.tril(np.ones((PAGE, PAGE), np.float32), -1), BF16)
    args = ([pool_fk] * npg + [pool_fv] * npg + [pool_lft] * npg + [pool_sk, pool_sv]
            + [qhat, khat, lfn, u3, u3, u3, u3, mst])
    return pl.pallas_call(
        functools.partial(_decode_kernel, layer=layer, n_new=n_new, n_pages=n_pages),
        grid_spec=grid_spec,
        out_shape=[jax.ShapeDtypeStruct((b, n_new, AW), F32), jax.ShapeDtypeStruct((b, n_new, AW), F32)],
        compiler_params=_cparams(("arbitrary", "arbitrary")),
        name="decode_attention",
    )(page_table, *args)


def _ssd_kernel(*refs, t_valid, has_init):
    if has_init:
        (xbc_ref, z_ref, sm_ref, h0_ref, cprev_ref, cw_ref, cb_ref, dtb_ref, alog_ref, dsk_ref, nw_ref,
         ex_ref, tri_ref, y_ref, hl_ref, ext_ref, ht_ref) = refs
    else:
        (xbc_ref, z_ref, sm_ref, cw_ref, cb_ref, dtb_ref, alog_ref, dsk_ref, nw_ref,
         ex_ref, tri_ref, y_ref, hl_ref, ext_ref, ht_ref) = refs
    T = SSM_CHUNK
    GW = SSM_W // SSM_GROUPS
    c = pl.program_id(1)

    @pl.when(c == 0)
    def _():
        if has_init:
            for g in range(SSM_GROUPS):
                ht_ref[g] = h0_ref[0, GW * g: GW * (g + 1), :].T
            ext_ref[0:SUBLANES, :] = cprev_ref[0]
        else:
            ht_ref[...] = jnp.zeros_like(ht_ref)
            ext_ref[0:SUBLANES, :] = jnp.zeros((SUBLANES, SSM_CONV_DIM), F32)

    if t_valid == T:
        ext_ref[SUBLANES:SUBLANES + T, :] = xbc_ref[...]
    else:
        ext_ref[SUBLANES:SUBLANES + T, :] = jnp.zeros((T, SSM_CONV_DIM), F32)
        ext_ref[SUBLANES:SUBLANES + t_valid, :] = xbc_ref[0]

    conv = cb_ref[...]
    for j in range(SSM_CONV):
        off = SUBLANES - (SSM_CONV - 1) + j
        conv = conv + ext_ref[off: off + T, :] * cw_ref[j:j + 1, :]
    act = _silu(conv)
    xs = act[:, :SSM_W]
    bm = act[:, SSM_W: SSM_W + SSM_GROUPS * SSM_STATE]
    cm = act[:, SSM_W + SSM_GROUPS * SSM_STATE:]

    sm = sm_ref[...] if t_valid == T else jnp.concatenate(
        [sm_ref[0], jnp.zeros((T - t_valid, LANES), F32)], axis=0)
    lane = lax.broadcasted_iota(jnp.int32, (T, LANES), 1)
    row = lax.broadcasted_iota(jnp.int32, (T, LANES), 0)
    live = (lane >= DT_LANE0) & (lane < DT_LANE0 + SSM_HEADS) & (row < t_valid)
    dt = jnp.where(live, _softplus(sm + dtb_ref[...]), 0.0)
    a = -jnp.exp(alog_ref[...])
    da = dt * a
    tri = tri_ref[...]
    p3 = _split3(da)
    acum = _dot(tri, p3[0]) + _dot(tri, p3[1]) + _dot(tri, p3[2])

    ex = ex_ref[...]
    acum_e = _dot3(_split3(acum), ex)
    dt_e = _dot3(_split3(dt), ex)
    total_e = acum_e[T - 1:T, :]
    exp_acum_e = jnp.exp(acum_e)
    wend_e = jnp.exp(total_e - acum_e) * dt_e
    cdecay_e = jnp.exp(total_e)

    xw = (xs * wend_e).astype(BF16)
    acum_t = acum.T
    dt_t = dt.T
    rr = lax.broadcasted_iota(jnp.int32, (T, T), 0)
    cc = lax.broadcasted_iota(jnp.int32, (T, T), 1)
    causal = rr >= cc
    xs_b = xs.astype(BF16)
    zero_b = jnp.zeros((T, LANES), BF16)

    y_parts = []
    for g in range(SSM_GROUPS):
        bg = bm[:, SSM_STATE * g: SSM_STATE * (g + 1)]
        cg = cm[:, SSM_STATE * g: SSM_STATE * (g + 1)].astype(BF16)
        cb = _dot_nt(cg, bg.astype(BF16))
        h_prev = ht_ref[g]
        yoff = _dot(cg, h_prev.astype(BF16)) * exp_acum_e[:, GW * g: GW * (g + 1)]
        st = _dot(bg.T.astype(BF16), xw[:, GW * g: GW * (g + 1)])
        ht_ref[g] = h_prev * cdecay_e[:, GW * g: GW * (g + 1)] + st
        hpg = SSM_HEADS // SSM_GROUPS
        for jp in range(hpg // 2):
            pair = (hpg // 2) * g + jp
            xp = xs_b[:, LANES * pair: LANES * (pair + 1)]
            yp = jnp.zeros((T, LANES), F32)
            for e in range(2):
                h = 2 * pair + e
                ln = DT_LANE0 + h
                seg = acum[:, ln:ln + 1] - acum_t[ln:ln + 1, :]
                dec = jnp.exp(jnp.where(causal, seg, NEG_INF))
                w = cb * dec * dt_t[ln:ln + 1, :]
                xh = jnp.where(_half_mask((T, LANES), e), xp, zero_b)
                yp = yp + _dot(w.astype(BF16), xh)
            y_parts.append(yp + yoff[:, LANES * jp: LANES * (jp + 1)])
    y = jnp.concatenate(y_parts, axis=1)
    y = y + xs * dsk_ref[...]
    zz = z_ref[...] if t_valid == T else jnp.concatenate(
        [z_ref[0], jnp.zeros((T - t_valid, SSM_W), F32)], axis=0)
    y = y * _silu(zz)
    outs = []
    for g in range(SSM_GROUPS):
        yg = y[:, GW * g: GW * (g + 1)]
        ms = jnp.mean(yg * yg, axis=-1, keepdims=True)
        outs.append(yg * lax.rsqrt(ms + NORM_EPS) * nw_ref[:, GW * g: GW * (g + 1)])
    yo = jnp.concatenate(outs, axis=1)
    if t_valid == T:
        y_ref[...] = yo
    else:
        y_ref[0] = yo[:t_valid, :]

    ext_ref[0:SUBLANES, :] = ext_ref[T: T + SUBLANES, :]

    @pl.when(c == pl.num_programs(1) - 1)
    def _():
        for g in range(SSM_GROUPS):
            hl_ref[0, GW * g: GW * (g + 1), :] = ht_ref[g].T


def _ssd_consts():
    ex = np.zeros((LANES, SSM_W), np.float32)
    for h in range(SSM_HEADS):
        ex[DT_LANE0 + h, SSM_HD * h: SSM_HD * (h + 1)] = 1.0
    tri = np.tril(np.ones((SSM_CHUNK, SSM_CHUNK), np.float32))
    return jnp.asarray(ex, BF16), jnp.asarray(tri, BF16)


def _ssd(u, b, s, params, h0=None, cprev=None):
    cw, cb, dtb, alog, dsk, nw = params
    ex, tri = _ssd_consts()
    T = SSM_CHUNK
    has_init = h0 is not None
    prompt = (s % T == 0)
    nc = s // T if prompt else 1
    t_valid = T if prompt else s

    def const(shape):
        return pl.BlockSpec(shape, lambda bi, c: (0,) * len(shape))

    if prompt:
        def ucol(col, w):
            return pl.BlockSpec((T, w), lambda bi, c, col=col, w=w: (bi * nc + c, col // w))
        y_spec = pl.BlockSpec((T, SSM_W), lambda bi, c: (bi * nc + c, 0))
        y_shape = jax.ShapeDtypeStruct((b * s, SSM_W), F32)
    else:
        def ucol(col, w):
            return pl.BlockSpec((1, s, w), lambda bi, c, col=col, w=w: (bi, 0, col // w))
        y_spec = pl.BlockSpec((1, s, SSM_W), lambda bi, c: (bi, 0, 0))
        y_shape = jax.ShapeDtypeStruct((b, s, SSM_W), F32)
    in_specs = [ucol(C_XBC, SSM_CONV_DIM), ucol(C_Z, SSM_W), ucol(C_SMALL, LANES)]
    args = [u, u, u]
    if has_init:
        in_specs += [pl.BlockSpec((1, SSM_W, SSM_STATE), lambda bi, c: (bi, 0, 0)),
                     pl.BlockSpec((1, SUBLANES, SSM_CONV_DIM), lambda bi, c: (bi, 0, 0))]
        args += [h0, cprev]
    in_specs += [const((SUBLANES, SSM_CONV_DIM)), const((1, SSM_CONV_DIM)), const((1, LANES)),
                 const((1, LANES)), const((1, SSM_W)), const((1, SSM_W)),
                 const((LANES, SSM_W)), const((T, T))]
    args += [cw, cb, dtb, alog, dsk, nw, ex, tri]
    return pl.pallas_call(
        functools.partial(_ssd_kernel, t_valid=t_valid, has_init=has_init),
        grid=(b, nc),
        in_specs=in_specs,
        out_specs=[y_spec, pl.BlockSpec((1, SSM_W, SSM_STATE), lambda bi, c: (bi, 0, 0))],
        out_shape=[y_shape, jax.ShapeDtypeStruct((b, SSM_W, SSM_STATE), F32)],
        scratch_shapes=[pltpu.VMEM((T + 2 * SUBLANES, SSM_CONV_DIM), F32),
                        pltpu.VMEM((SSM_GROUPS, SSM_STATE, SSM_W // SSM_GROUPS), F32)],
        compiler_params=_cparams(("arbitrary", "arbitrary")),
        name="ssd",
    )(*args)


def _merge_kernel(x_ref, oa_ref, fg_ref, ob_ref, sg_ref, yc_ref, ga_ref, gb_ref, gc_ref, bg_ref,
                  wpa_ref, wpb_ref, wpc_ref, wout_ref, out_ref):
    ya = (oa_ref[...] * _silu(fg_ref[...])).astype(BF16)
    yb = (ob_ref[...] * _silu(sg_ref[...])).astype(BF16)
    yc = yc_ref[...].astype(BF16)
    h = _sigmoid(ga_ref[...] + bg_ref[:, 0:D_MODEL]) * _dot(ya, wpa_ref[...])
    h = h + _sigmoid(gb_ref[...] + bg_ref[:, D_MODEL:2 * D_MODEL]) * _dot(yb, wpb_ref[...])
    h = h + _sigmoid(gc_ref[...] + bg_ref[:, 2 * D_MODEL:3 * D_MODEL]) * _dot(yc, wpc_ref[...])
    out_ref[...] = x_ref[...] + _dot(h.astype(BF16), wout_ref[...])


def _merge(x2d, u, oa, ob, yc, bg, wpa, wpb, wpc, wout, tm):
    n = x2d.shape[0]

    def rows(w, col=0):
        return pl.BlockSpec((tm, w), lambda i, col=col, w=w: (i, col // w))

    def const(shape):
        return pl.BlockSpec(shape, lambda i: (0,) * len(shape))

    return pl.pallas_call(
        _merge_kernel,
        grid=(n // tm,),
        in_specs=[rows(D_MODEL), rows(AW), rows(AW, C_FG), rows(AW), rows(AW, C_SG), rows(SSM_W),
                  rows(D_MODEL, C_GL), rows(D_MODEL, C_GL + D_MODEL), rows(D_MODEL, C_GL + 2 * D_MODEL),
                  const((1, 3 * D_MODEL)), const((AW, D_MODEL)), const((AW, D_MODEL)),
                  const((SSM_W, D_MODEL)), const((D_MODEL, D_MODEL))],
        out_specs=rows(D_MODEL),
        out_shape=jax.ShapeDtypeStruct((n, D_MODEL), F32),
        compiler_params=_cparams(("arbitrary",)),
        name="merge",
    )(x2d, oa, u, ob, u, yc, u, u, u, bg, wpa, wpb, wpc, wout)


def _regroup_w_in(w):
    d = w.shape[0]
    z = lambda n: jnp.zeros((d, n), w.dtype)
    fox = jnp.concatenate([w[:, 0:1536], w[:, 1544:2056]], axis=1)
    sb = w[:, 2056:4104]
    cz = w[:, 4104:5128]
    cxbc = w[:, 5128:6664]
    cdt = w[:, 6664:6680]
    gl = w[:, 6680:9752]
    ff = w[:, 1536:1544]
    small = jnp.concatenate([ff, cdt, z(C_XBC - C_SMALL - 24)], axis=1)
    return jnp.concatenate([fox, sb, small, cxbc, cz, gl], axis=1).astype(BF16)


def _pad_lanes(v, lane0, width=LANES):
    out = jnp.zeros((1, width), F32)
    return out.at[0, lane0: lane0 + v.shape[0]].set(v.astype(F32))


def kernel(x_prompt, x_sample, cache_fox_k, cache_fox_v, cache_fox_logf, cache_sb_k, cache_sb_v,
           state_ssm, state_conv, page_table, norm_g, w_in, b_forget, fox_q_norm, fox_k_norm,
           conv_w, conv_b, dt_bias, a_log, d_skip, ssm_norm, b_gate, w_pa, w_pb, w_pc, w_out):
    depth = norm_g.shape[0]
    bp, sp, _ = x_prompt.shape
    bs, ss, _ = x_sample.shape
    n_pool = cache_fox_k.shape[1]
    npr, nsm = bp * sp, bs * ss

    def pool_view(c):
        return jnp.transpose(c, (0, 1, 3, 4, 2)).reshape(depth, n_pool, AW, PAGE)

    def cache_leaf(t):
        return jnp.transpose(t.reshape(t.shape[0], HEADS, HD, t.shape[2]), (0, 3, 1, 2))

    pool_fk, pool_fv, pool_sk, pool_sv = (pool_view(c) for c in
                                          (cache_fox_k, cache_fox_v, cache_sb_k, cache_sb_v))
    pool_lft = jnp.swapaxes(cache_fox_logf, 2, 3)
    h0_all = state_ssm.reshape(depth, bs, SSM_W, SSM_STATE)
    cprev_all = jnp.pad(state_conv, ((0, 0), (0, 0), (SUBLANES - (SSM_CONV - 1), 0), (0, 0)))

    xp = x_prompt.reshape(npr, D_MODEL)
    xs = x_sample.reshape(nsm, D_MODEL)
    new_p, new_s = [], []
    for l in range(depth):
        w_pad = _regroup_w_in(w_in[l])
        g = norm_g[l].reshape(1, D_MODEL)
        gq = jnp.tile(fox_q_norm[l], HEADS).reshape(1, AW) * (HD ** -0.5)
        gq2 = gq * LOG2E
        gk = jnp.tile(fox_k_norm[l], HEADS).reshape(1, AW)
        bf = _pad_lanes(b_forget[l], 0)
        ssd_params = (jnp.pad(conv_w[l], ((0, SUBLANES - SSM_CONV), (0, 0))), conv_b[l].reshape(1, -1),
                      _pad_lanes(dt_bias[l], DT_LANE0), _pad_lanes(a_log[l], DT_LANE0),
                      jnp.repeat(d_skip[l], SSM_HD).reshape(1, SSM_W), ssm_norm[l].reshape(1, SSM_W))
        bg = b_gate[l].reshape(1, -1)
        wpa, wpb, wpc, wout = (w_pa[l].astype(BF16), w_pb[l].astype(BF16), w_pc[l].astype(BF16),
                               w_out[l].astype(BF16))

        u = _in_proj(xp, g, w_pad, tm=min(1024, npr))
        kt, vt, skt, svt, lft, qa, ka, vtf, qs, ks, vts = _prep_prompt(u, bp, sp, gq2, gk, bf)
        oa = _fox_attention(qa, ka, vtf).reshape(npr, AW)
        ob = _sb_attention(qs, ks, vts).reshape(npr, AW)
        yc, h_last = _ssd(u, bp, sp, ssd_params)
        u3 = u.reshape(bp, sp, U_W)
        new_p.append((cache_leaf(kt), cache_leaf(vt), jnp.transpose(lft, (0, 2, 1)),
                      cache_leaf(skt), cache_leaf(svt),
                      h_last.reshape(bp, SSM_HEADS, SSM_HD, SSM_STATE),
                      u3[:, sp - (SSM_CONV - 1):, C_XBC:C_XBC + SSM_CONV_DIM]))
        xp = _merge(xp, u, oa, ob, yc, bg, wpa, wpb, wpc, wout, tm=min(512, npr))

        us = _in_proj(xs, g, w_pad, tm=nsm)
        qhat_s, khat_s, logf_s = _prep_sample(us, gq, gk, bf)
        us3 = us.reshape(bs, ss, U_W)
        oa_s, ob_s = _decode_attention(l, page_table, pool_fk, pool_fv, pool_lft, pool_sk, pool_sv,
                                       qhat_s.reshape(bs, ss, AW), khat_s.reshape(bs, ss, AW),
                                       logf_s.reshape(bs, ss, LANES), us3)
        yc_s, h_last_s = _ssd(us3, bs, ss, ssd_params, h0=h0_all[l], cprev=cprev_all[l])
        conv_cat = jnp.concatenate([state_conv[l], us3[:, :, C_XBC:C_XBC + SSM_CONV_DIM]], axis=1)
        new_s.append((khat_s.reshape(bs, ss, HEADS, HD),
                      us3[:, :, C_FV:C_FV + AW].reshape(bs, ss, HEADS, HD),
                      logf_s[:, :HEADS].reshape(bs, ss, HEADS),
                      us3[:, :, C_SK:C_SK + AW].reshape(bs, ss, HEADS, HD),
                      us3[:, :, C_SV:C_SV + AW].reshape(bs, ss, HEADS, HD),
                      h_last_s.reshape(bs, SSM_HEADS, SSM_HD, SSM_STATE),
                      conv_cat[:, -(SSM_CONV - 1):]))
        xs = _merge(xs, us, oa_s.reshape(nsm, AW), ob_s.reshape(nsm, AW), yc_s.reshape(nsm, SSM_W),
                    bg, wpa, wpb, wpc, wout, tm=nsm)

    outs_p = [jnp.stack([st[i] for st in new_p]) for i in range(7)]
    outs_s = [jnp.stack([st[i] for st in new_s]) for i in range(7)]
    return (xp.reshape(bp, sp, D_MODEL), xs.reshape(bs, ss, D_MODEL), *outs_p, *outs_s)
```
